```python
import jax, jax.numpy as jnp
from jax import lax
import numpy as np

D_MODEL = 4096
BATCH = 8
SEQ = 2048
DEPTH = 2

MIX_WIDTH = D_MODEL
W_A = MIX_WIDTH // 4
W_B = MIX_WIDTH // 4
W_C = MIX_WIDTH // 4
W_D = MIX_WIDTH // 4
POOL_WINDOWS = (2, 4, 8, 16)
N_POOL_GROUPS = len(POOL_WINDOWS)
POOL_GROUP_DIM = W_A // N_POOL_GROUPS
CHUNK = 128
SGU_HEAD_DIM = 128
SGU_HEADS = W_B // SGU_HEAD_DIM
SHORT_CONV_K = 3
CONF_CONV_K = 31
N_BRANCH = 4
FFN_HIDDEN = 11008
FFN_CONV_K = 3
N_IN = W_A + 2 * W_B + 3 * W_C + 2 * W_D + N_BRANCH * D_MODEL
IN_SPLITS = (W_A, W_A + 2 * W_B, W_A + 2 * W_B + 3 * W_C, W_A + 2 * W_B + 3 * W_C + 2 * W_D)
BRANCH_ROWS = ((0, W_A), (W_A, W_A + W_B), (W_A + W_B, W_A + W_B + W_C), (W_A + W_B + W_C, MIX_WIDTH))
RMS_EPS = 1e-6
LN_EPS = 1e-5

kernel_name = "hybrid_pool_sgu_shortconv_conformer_convffn"


def rms_norm(x, g):
    xf = x.astype(jnp.float32)
    y = xf * lax.rsqrt(jnp.mean(xf * xf, axis=-1, keepdims=True) + RMS_EPS)
    return (y * g.astype(jnp.float32)).astype(x.dtype)


def layer_norm(x, g, b):
    xf = x.astype(jnp.float32)
    mu = jnp.mean(xf, axis=-1, keepdims=True)
    xc = xf - mu
    var = jnp.mean(xc * xc, axis=-1, keepdims=True)
    y = xc * lax.rsqrt(var + LN_EPS)
    return (y * g.astype(jnp.float32) + b.astype(jnp.float32)).astype(x.dtype)


def causal_dwconv(x, w):
    k, ch = w.shape
    return lax.conv_general_dilated(
        x, w[:, None, :].astype(x.dtype), window_strides=(1,), padding=[(k - 1, 0)],
        dimension_numbers=('NWC', 'WIO', 'NWC'), feature_group_count=ch)


def pool_mixer(a, pool_w, pool_scale):
    bsz, s, _ = a.shape
    af = a.astype(jnp.float32).reshape(bsz, s, N_POOL_GROUPS, POOL_GROUP_DIM)
    csum = jnp.cumsum(af, axis=1)
    t = jnp.arange(s)
    outs = []
    for g, w in enumerate(POOL_WINDOWS):
        cg = csum[:, :, g]
        lag = jnp.pad(cg, ((0, 0), (w, 0), (0, 0)))[:, :s]
        cnt = jnp.minimum(t + 1, w).astype(jnp.float32)[None, :, None]
        outs.append((cg - lag) / cnt - af[:, :, g])
    pooled = jnp.stack(outs, axis=2).astype(a.dtype)
    y = jnp.einsum('bsgc,gcd->bsgd', pooled, pool_w).reshape(bsz, s, W_A)
    return y * pool_scale


def sgu_mixer(z, ln_g, ln_b, w_s, b_s):
    z = jax.nn.gelu(z, approximate=False)
    u, v = jnp.split(z, 2, axis=-1)
    v = layer_norm(v, ln_g, ln_b)
    bsz, s, _ = v.shape
    v = v.reshape(bsz, s // CHUNK, CHUNK, SGU_HEADS, SGU_HEAD_DIM)
    mask = jnp.tril(jnp.ones((CHUNK, CHUNK), dtype=bool))
    w = jnp.where(mask[None], w_s, 0)
    sv = jnp.einsum('hts,bnshc->bnthc', w, v) + jnp.transpose(b_s)[None, None, :, :, None]
    return u * sv.reshape(bsz, s, W_B)


def short_conv_mixer(z, conv_w):
    bg, cg, hx = jnp.split(z, 3, axis=-1)
    return bg * causal_dwconv(cg * hx, conv_w)


def conformer_conv_mixer(z, dw_w, dw_b, ln_g, ln_b):
    a, g = jnp.split(z, 2, axis=-1)
    y = a * jax.nn.sigmoid(g)
    y = causal_dwconv(y, dw_w) + dw_b
    y = layer_norm(y, ln_g, ln_b)
    return jax.nn.silu(y)


def token_mix(h, w_in, pool_w, pool_scale, sgu_ln_g, sgu_ln_b, sgu_w, sgu_b, sconv_w,
              conf_dw_w, conf_dw_b, conf_ln_g, conf_ln_b, gate_b, w_branch, w_out):
    bsz, s, _ = h.shape
    proj = h @ w_in
    a_in, b_in, c_in, d_in, g_in = jnp.split(proj, IN_SPLITS, axis=-1)
    ys = (pool_mixer(a_in, pool_w, pool_scale),
          sgu_mixer(b_in, sgu_ln_g, sgu_ln_b, sgu_w, sgu_b),
          short_conv_mixer(c_in, sconv_w),
          conformer_conv_mixer(d_in, conf_dw_w, conf_dw_b, conf_ln_g, conf_ln_b))
    gates = jax.nn.sigmoid(g_in + gate_b).reshape(bsz, s, N_BRANCH, D_MODEL)
    merged = None
    for i, (y, (r0, r1)) in enumerate(zip(ys, BRANCH_ROWS)):
        term = gates[:, :, i] * (y @ w_branch[r0:r1])
        merged = term if merged is None else merged + term
    return merged @ w_out


def conv_ffn(h, up, conv_w, down):
    z = causal_dwconv(h @ up, conv_w)
    gate, val = jnp.split(z, 2, axis=-1)
    return (jax.nn.silu(gate) * val) @ down


def setup_inputs(seed: int = 0) -> dict:
    key = jax.random.key(seed)
    ks = jax.random.split(key, 26)
    f32 = jnp.float32

    def nrm(k, shape, scale):
        return jax.random.normal(k, shape, f32) * scale

    L, D = DEPTH, D_MODEL
    return {
        "x": nrm(ks[0], (BATCH, SEQ, D), 1.0),
        "c": nrm(ks[1], (BATCH, D), 1.0),
        "ada_w": nrm(ks[2], (L, D, 6 * D), D ** -0.5),
        "ada_b": nrm(ks[3], (L, 6 * D), 0.02),
        "norm_mix_g": 1.0 + nrm(ks[4], (L, D), 0.02),
        "w_in": nrm(ks[5], (L, D, N_IN), D ** -0.5),
        "pool_w": nrm(ks[6], (L, N_POOL_GROUPS, POOL_GROUP_DIM, POOL_GROUP_DIM), POOL_GROUP_DIM ** -0.5),
        "pool_scale": 1.0 + nrm(ks[7], (L, W_A), 0.02),
        "sgu_ln_g": 1.0 + nrm(ks[8], (L, W_B), 0.02),
        "sgu_ln_b": nrm(ks[9], (L, W_B), 0.02),
        "sgu_w": nrm(ks[10], (L, SGU_HEADS, CHUNK, CHUNK), CHUNK ** -0.5),
        "sgu_b": 1.0 + nrm(ks[11], (L, SGU_HEADS, CHUNK), 0.02),
        "sconv_w": nrm(ks[12], (L, SHORT_CONV_K, W_C), SHORT_CONV_K ** -0.5),
        "conf_dw_w": nrm(ks[13], (L, CONF_CONV_K, W_D), CONF_CONV_K ** -0.5),
        "conf_dw_b": nrm(ks[14], (L, W_D), 0.02),
        "conf_ln_g": 1.0 + nrm(ks[15], (L, W_D), 0.02),
        "conf_ln_b": nrm(ks[16], (L, W_D), 0.02),
        "gate_b": nrm(ks[17], (L, N_BRANCH * D), 0.02),
        "w_branch": nrm(ks[18], (L, MIX_WIDTH, D), MIX_WIDTH ** -0.5),
        "w_out": nrm(ks[19], (L, D, D), D ** -0.5),
        "norm_ffn_g": 1.0 + nrm(ks[20], (L, D), 0.02),
        "ffn_up": nrm(ks[21], (L, D, 2 * FFN_HIDDEN), D ** -0.5),
        "ffn_conv": nrm(ks[22], (L, FFN_CONV_K, 2 * FFN_HIDDEN), FFN_CONV_K ** -0.5),
        "ffn_down": nrm(ks[23], (L, FFN_HIDDEN, D), FFN_HIDDEN ** -0.5),
        "final_g": 1.0 + nrm(ks[24], (D,), 0.02),
    }


def reference(x, c, ada_w, ada_b, norm_mix_g, w_in, pool_w, pool_scale, sgu_ln_g, sgu_ln_b,
              sgu_w, sgu_b, sconv_w, conf_dw_w, conf_dw_b, conf_ln_g, conf_ln_b, gate_b,
              w_branch, w_out, norm_ffn_g, ffn_up, ffn_conv, ffn_down, final_g):
    cond = jax.nn.silu(c)
    for l in range(DEPTH):
        mod = (cond @ ada_w[l] + ada_b[l])[:, None, :]
        sh_m, sc_m, g_m, sh_f, sc_f, g_f = jnp.split(mod, 6, axis=-1)
        h = rms_norm(x, norm_mix_g[l]) * (1.0 + sc_m) + sh_m
        x = x + g_m * token_mix(h, w_in[l], pool_w[l], pool_scale[l], sgu_ln_g[l], sgu_ln_b[l],
                                sgu_w[l], sgu_b[l], sconv_w[l], conf_dw_w[l], conf_dw_b[l],
                                conf_ln_g[l], conf_ln_b[l], gate_b[l], w_branch[l], w_out[l])
        h = rms_norm(x, norm_ffn_g[l]) * (1.0 + sc_f) + sh_f
        x = x + g_f * conv_ffn(h, ffn_up[l], ffn_conv[l], ffn_down[l])
    return rms_norm(x, final_g)
```

```python
import functools

import jax
import jax.numpy as jnp
from jax import lax
from jax.experimental import pallas as pl
from jax.experimental.pallas import tpu as pltpu

F32 = jnp.float32
BF16 = jnp.bfloat16

D_MODEL = 4096
SEQ = 2048
W_MIX = 1024
N_MIX_IN = 8 * W_MIX
POOL_WINDOWS = (2, 4, 8, 16)
POOL_GROUP = W_MIX // len(POOL_WINDOWS)
CHUNK = 128
SGU_HEADS = W_MIX // CHUNK
SHORT_K = 3
CONF_K = 31
N_BRANCH = 4
FFN_HIDDEN = 11008
FFN_PAD = 11264
RMS_EPS = 1e-6
LN_EPS = 1e-5

V7X_VMEM_LIMIT = 60000 * 1024
SUBLANES = 8


def _cparams(sem):
    return pltpu.CompilerParams(dimension_semantics=sem, vmem_limit_bytes=V7X_VMEM_LIMIT)


def _ada_kernel(c_ref, w_ref, b_ref, o_ref):
    cond = jax.nn.silu(c_ref[...]).astype(BF16)
    o_ref[...] = jnp.dot(cond, w_ref[...].astype(BF16), preferred_element_type=F32) + b_ref[...]


def _ada(c, ada_w, ada_b, tn=1024):
    n_layers, d, n = ada_w.shape
    bsz = c.shape[0]
    return pl.pallas_call(
        _ada_kernel,
        grid=(n_layers, n // tn),
        in_specs=[
            pl.BlockSpec((bsz, d), lambda l, j: (0, 0)),
            pl.BlockSpec((None, d, tn), lambda l, j: (l, 0, j)),
            pl.BlockSpec((None, 1, tn), lambda l, j: (l, 0, j)),
        ],
        out_specs=pl.BlockSpec((None, bsz, tn), lambda l, j: (l, 0, j)),
        out_shape=jax.ShapeDtypeStruct((n_layers, bsz, n), F32),
        compiler_params=_cparams(("arbitrary", "arbitrary")),
        name="ada_mod",
    )(c, ada_w, ada_b.reshape(n_layers, 1, n))


def _norm_mod_kernel(x_ref, g_ref, sc_ref, sh_ref, o_ref):
    x = x_ref[...]
    y = x * lax.rsqrt(jnp.mean(x * x, axis=-1, keepdims=True) + RMS_EPS) * g_ref[...]
    o_ref[...] = (y * (1.0 + sc_ref[...]) + sh_ref[...]).astype(o_ref.dtype)


def _norm_kernel(x_ref, g_ref, o_ref):
    x = x_ref[...]
    y = x * lax.rsqrt(jnp.mean(x * x, axis=-1, keepdims=True) + RMS_EPS) * g_ref[...]
    o_ref[...] = y.astype(o_ref.dtype)


def _norm_mod(x2, g, mod3, sc_idx, sh_idx, tr=256):
    t, d = x2.shape
    per_seq = SEQ // tr
    return pl.pallas_call(
        _norm_mod_kernel,
        grid=(t // tr,),
        in_specs=[
            pl.BlockSpec((tr, d), lambda i: (i, 0)),
            pl.BlockSpec((1, d), lambda i: (0, 0)),
            pl.BlockSpec((None, 1, d), lambda i: (i // per_seq, 0, sc_idx)),
            pl.BlockSpec((None, 1, d), lambda i: (i // per_seq, 0, sh_idx)),
        ],
        out_specs=pl.BlockSpec((tr, d), lambda i: (i, 0)),
        out_shape=jax.ShapeDtypeStruct((t, d), BF16),
        compiler_params=_cparams(("arbitrary",)),
        name="norm_mod",
    )(x2, g.reshape(1, d), mod3, mod3)


def _final_norm(x2, g, tr=256):
    t, d = x2.shape
    return pl.pallas_call(
        _norm_kernel,
        grid=(t // tr,),
        in_specs=[
            pl.BlockSpec((tr, d), lambda i: (i, 0)),
            pl.BlockSpec((1, d), lambda i: (0, 0)),
        ],
        out_specs=pl.BlockSpec((tr, d), lambda i: (i, 0)),
        out_shape=jax.ShapeDtypeStruct((t, d), F32),
        compiler_params=_cparams(("arbitrary",)),
        name="final_norm",
    )(x2, g.reshape(1, d))


def _mm_kernel(a_ref, b_ref, o_ref):
    o_ref[...] = jnp.dot(a_ref[...], b_ref[...], preferred_element_type=F32).astype(o_ref.dtype)


def _mix_proj(h, w_in_b, tm=1024, tn=512):
    t, d = h.shape
    return pl.pallas_call(
        _mm_kernel,
        grid=(t // tm, N_MIX_IN // tn),
        in_specs=[
            pl.BlockSpec((tm, d), lambda i, j: (i, 0)),
            pl.BlockSpec((d, tn), lambda i, j: (0, j)),
        ],
        out_specs=pl.BlockSpec((tm, tn), lambda i, j: (i, j)),
        out_shape=jax.ShapeDtypeStruct((t, N_MIX_IN), F32),
        compiler_params=_cparams(("arbitrary", "arbitrary")),
        name="mix_proj",
    )(h, w_in_b)


A_HALO = 16
Q_HALO = 8
D_HALO = 32
CONV_ROWS = 32


def _carry_halo(buf, halo, tt, first):
    @pl.when(first)
    def _():
        buf[0:halo, :] = jnp.zeros((halo, buf.shape[1]), buf.dtype)

    @pl.when(jnp.logical_not(first))
    def _():
        buf[0:halo, :] = buf[tt:tt + halo, :]


def _gelu_exact(x):
    return 0.5 * x * (1.0 + lax.erf(x * (2.0 ** -0.5)))


def _layer_norm(v, g, b):
    mu = jnp.mean(v, axis=-1, keepdims=True)
    vc = v - mu
    var = jnp.mean(vc * vc, axis=-1, keepdims=True)
    return vc * lax.rsqrt(var + LN_EPS) * g + b


def _mixer_kernel(p_ref, poolw_ref, pools_ref, slng_ref, slnb_ref, sw_ref, sbt_ref, scw_ref,
                  dww_ref, dwb_ref, clng_ref, clnb_ref, y_ref, abuf, qbuf, dbuf, sbuf):
    tt = p_ref.shape[0]
    per_seq = SEQ // tt
    i = pl.program_id(0)
    first = (i % per_seq) == 0
    t0 = (i % per_seq) * tt

    _carry_halo(abuf, A_HALO, tt, first)
    abuf[A_HALO:A_HALO + tt, :] = p_ref[:, 0:W_MIX]
    pos = (lax.broadcasted_iota(jnp.int32, (tt, 1), 0) + t0 + 1).astype(F32)
    for g, w in enumerate(POOL_WINDOWS):
        c0 = g * POOL_GROUP
        cur = abuf[A_HALO:A_HALO + tt, c0:c0 + POOL_GROUP]
        win = cur
        for j in range(1, w):
            win = win + abuf[A_HALO - j:A_HALO - j + tt, c0:c0 + POOL_GROUP]
        pooled = win / jnp.minimum(pos, float(w)) - cur
        ya = jnp.dot(pooled.astype(BF16), poolw_ref[g], preferred_element_type=F32)
        y_ref[:, c0:c0 + POOL_GROUP] = (ya * pools_ref[:, c0:c0 + POOL_GROUP]).astype(y_ref.dtype)

    row = lax.broadcasted_iota(jnp.int32, (CHUNK, CHUNK), 0)
    col = lax.broadcasted_iota(jnp.int32, (CHUNK, CHUNK), 1)
    tril = row >= col
    for c in range(tt // CHUNK):
        r0 = c * CHUNK
        z = _gelu_exact(p_ref[r0:r0 + CHUNK, W_MIX:3 * W_MIX])
        u = z[:, :W_MIX]
        v = _layer_norm(z[:, W_MIX:], slng_ref[...], slnb_ref[...]).astype(BF16)
        for hd in range(SGU_HEADS):
            h0 = hd * CHUNK
            wm = jnp.where(tril, sw_ref[hd], 0.0).astype(BF16)
            sv = jnp.dot(wm, v[:, h0:h0 + CHUNK], preferred_element_type=F32) + sbt_ref[hd]
            y_ref[r0:r0 + CHUNK, W_MIX + h0:W_MIX + h0 + CHUNK] = (
                u[:, h0:h0 + CHUNK] * sv).astype(y_ref.dtype)

    _carry_halo(qbuf, Q_HALO, tt, first)
    q = p_ref[:, 4 * W_MIX:5 * W_MIX] * p_ref[:, 5 * W_MIX:6 * W_MIX]
    qbuf[Q_HALO:Q_HALO + tt, :] = q
    conv = scw_ref[SHORT_K - 1:SHORT_K, :] * q
    for k in range(SHORT_K - 1):
        s = SHORT_K - 1 - k
        conv = conv + scw_ref[k:k + 1, :] * qbuf[Q_HALO - s:Q_HALO - s + tt, :]
    y_ref[:, 2 * W_MIX:3 * W_MIX] = (p_ref[:, 3 * W_MIX:4 * W_MIX] * conv).astype(y_ref.dtype)

    _carry_halo(dbuf, D_HALO, tt, first)
    dbuf[D_HALO:D_HALO + tt, :] = (p_ref[:, 6 * W_MIX:7 * W_MIX]
                                   * jax.nn.sigmoid(p_ref[:, 7 * W_MIX:8 * W_MIX]))
    n_s = tt + D_HALO
    for r in range(SUBLANES):
        sbuf[r, 0:n_s - r, :] = dbuf[r:n_s, :]

    def conv_rows(ci, carry):
        r0 = pl.multiple_of(ci * CONV_ROWS, CONV_ROWS)
        acc = jnp.broadcast_to(dwb_ref[...], (CONV_ROWS, W_MIX))
        for k in range(CONF_K):
            qq, rr = divmod(k + D_HALO - (CONF_K - 1), SUBLANES)
            acc = acc + dww_ref[k:k + 1, :] * sbuf[rr, pl.ds(r0 + qq * SUBLANES, CONV_ROWS), :]
        yd = _layer_norm(acc, clng_ref[...], clnb_ref[...])
        y_ref[pl.ds(r0, CONV_ROWS), 3 * W_MIX:4 * W_MIX] = jax.nn.silu(yd).astype(y_ref.dtype)
        return carry

    lax.fori_loop(0, tt // CONV_ROWS, conv_rows, 0)


def _mixers(p, pool_w_b, pool_scale, sgu_ln_g, sgu_ln_b, sgu_w, sgu_b_t, sconv_w,
            conf_dw_w, conf_dw_b, conf_ln_g, conf_ln_b, tt=256):
    t = p.shape[0]

    def full(a):
        nd = a.ndim
        return pl.BlockSpec(a.shape, lambda i: (0,) * nd)

    small = (pool_w_b, pool_scale, sgu_ln_g, sgu_ln_b, sgu_w, sgu_b_t, sconv_w,
             conf_dw_w, conf_dw_b, conf_ln_g, conf_ln_b)
    return pl.pallas_call(
        _mixer_kernel,
        grid=(t // tt,),
        in_specs=[pl.BlockSpec((tt, N_MIX_IN), lambda i: (i, 0))] + [full(a) for a in small],
        out_specs=pl.BlockSpec((tt, N_BRANCH * W_MIX), lambda i: (i, 0)),
        out_shape=jax.ShapeDtypeStruct((t, N_BRANCH * W_MIX), BF16),
        scratch_shapes=[
            pltpu.VMEM((A_HALO + tt, W_MIX), F32),
            pltpu.VMEM((Q_HALO + tt, W_MIX), F32),
            pltpu.VMEM((D_HALO + tt, W_MIX), F32),
            pltpu.VMEM((SUBLANES, D_HALO + tt, W_MIX), F32),
        ],
        compiler_params=_cparams(("arbitrary",)),
        name="mixers",
    )(p, *small)


def _gate_lift_kernel(h_ref, y_ref, wg_ref, wb_ref, gb_ref, o_ref, acc_ref):
    br = pl.program_id(2)
    gate = jnp.dot(h_ref[...], wg_ref[...], preferred_element_type=F32) + gb_ref[...]
    lift = jnp.dot(y_ref[...], wb_ref[...], preferred_element_type=F32)
    term = jax.nn.sigmoid(gate) * lift

    @pl.when(br == 0)
    def _():
        acc_ref[...] = term

    @pl.when(br > 0)
    def _():
        acc_ref[...] = acc_ref[...] + term

    @pl.when(br == N_BRANCH - 1)
    def _():
        o_ref[...] = acc_ref[...].astype(o_ref.dtype)


def _gate_lift(h, y, w_in_b, w_branch_b, gate_b, tm=1024, tn=512):
    t, d = h.shape
    nj = d // tn
    gate_col0 = N_MIX_IN // tn
    return pl.pallas_call(
        _gate_lift_kernel,
        grid=(t // tm, nj, N_BRANCH),
        in_specs=[
            pl.BlockSpec((tm, d), lambda i, j, b: (i, 0)),
            pl.BlockSpec((tm, W_MIX), lambda i, j, b: (i, b)),
            pl.BlockSpec((d, tn), lambda i, j, b: (0, gate_col0 + b * nj + j)),
            pl.BlockSpec((W_MIX, tn), lambda i, j, b: (b, j)),
            pl.BlockSpec((1, tn), lambda i, j, b: (0, b * nj + j)),
        ],
        out_specs=pl.BlockSpec((tm, tn), lambda i, j, b: (i, j)),
        out_shape=jax.ShapeDtypeStruct((t, d), BF16),
        scratch_shapes=[pltpu.VMEM((tm, tn), F32)],
        compiler_params=_cparams(("arbitrary", "arbitrary", "arbitrary")),
        name="gate_lift",
    )(h, y, w_in_b, w_branch_b, gate_b.reshape(1, -1))


def _resid_mm_kernel(a_ref, w_ref, x_ref, g_ref, o_ref, acc_ref, *, nk):
    k = pl.program_id(2)
    part = jnp.dot(a_ref[...], w_ref[...], preferred_element_type=F32)
    if nk == 1:
        o_ref[...] = x_ref[...] + g_ref[...] * part
        return

    @pl.when(k == 0)
    def _():
        acc_ref[...] = part

    @pl.when(jnp.logical_and(k > 0, k < nk - 1))
    def _():
        acc_ref[...] = acc_ref[...] + part

    @pl.when(k == nk - 1)
    def _():
        o_ref[...] = x_ref[...] + g_ref[...] * (acc_ref[...] + part)


def _resid_mm(a, w, x2, mod3, g_idx, tm, tn, tk):
    t, kdim = a.shape
    d = w.shape[1]
    nk = kdim // tk
    per_seq = SEQ // tm
    nj = d // tn
    return pl.pallas_call(
        functools.partial(_resid_mm_kernel, nk=nk),
        grid=(t // tm, nj, nk),
        in_specs=[
            pl.BlockSpec((tm, tk), lambda i, j, k: (i, k)),
            pl.BlockSpec((tk, tn), lambda i, j, k: (k, j)),
            pl.BlockSpec((tm, tn), lambda i, j, k: (i, j)),
            pl.BlockSpec((None, 1, tn), lambda i, j, k: (i // per_seq, 0, g_idx * nj + j)),
        ],
        out_specs=pl.BlockSpec((tm, tn), lambda i, j, k: (i, j)),
        out_shape=jax.ShapeDtypeStruct((t, d), F32),
        scratch_shapes=[pltpu.VMEM((tm, tn), F32)],
        compiler_params=_cparams(("arbitrary", "arbitrary", "arbitrary")),
        name="resid_mm",
    )(a, w, x2, mod3)


Z_HALO = 8
FFN_CONV_ROWS = 3


def _ffn_up_kernel(h_ref, wg_ref, wv_ref, cg_ref, cv_ref, o_ref, gbuf, vbuf):
    tm = h_ref.shape[0]
    per_seq = SEQ // tm
    first = (pl.program_id(1) % per_seq) == 0
    h = h_ref[...]

    def conv_branch(w_ref, c_ref, buf):
        z = jnp.dot(h, w_ref[...], preferred_element_type=F32)
        _carry_halo(buf, Z_HALO, tm, first)
        buf[Z_HALO:Z_HALO + tm, :] = z
        out = c_ref[2:3, :] * z
        out = out + c_ref[1:2, :] * buf[Z_HALO - 1:Z_HALO - 1 + tm, :]
        out = out + c_ref[0:1, :] * buf[Z_HALO - 2:Z_HALO - 2 + tm, :]
        return out

    gate = conv_branch(wg_ref, cg_ref, gbuf)
    val = conv_branch(wv_ref, cv_ref, vbuf)
    o_ref[...] = (jax.nn.silu(gate) * val).astype(o_ref.dtype)


def _ffn_up(h, up_b, conv_p, tm=1024, tn=512):
    t, d = h.shape
    nj = FFN_PAD // tn
    return pl.pallas_call(
        _ffn_up_kernel,
        grid=(nj, t // tm),
        in_specs=[
            pl.BlockSpec((tm, d), lambda j, i: (i, 0)),
            pl.BlockSpec((d, tn), lambda j, i: (0, j)),
            pl.BlockSpec((d, tn), lambda j, i: (0, nj + j)),
            pl.BlockSpec((FFN_CONV_ROWS, tn), lambda j, i: (0, j)),
            pl.BlockSpec((FFN_CONV_ROWS, tn), lambda j, i: (0, nj + j)),
        ],
        out_specs=pl.BlockSpec((tm, tn), lambda j, i: (i, j)),
        out_shape=jax.ShapeDtypeStruct((t, FFN_PAD), BF16),
        scratch_shapes=[pltpu.VMEM((Z_HALO + tm, tn), F32), pltpu.VMEM((Z_HALO + tm, tn), F32)],
        compiler_params=_cparams(("arbitrary", "arbitrary")),
        name="ffn_up",
    )(h, up_b, up_b, conv_p, conv_p)


def _pad_cols(a, n_to):
    return jnp.pad(a, ((0, 0), (0, n_to - a.shape[1])))


def kernel(x, c, ada_w, ada_b, norm_mix_g, w_in, pool_w, pool_scale, sgu_ln_g, sgu_ln_b, sgu_w,
           sgu_b, sconv_w, conf_dw_w, conf_dw_b, conf_ln_g, conf_ln_b, gate_b, w_branch, w_out,
           norm_ffn_g, ffn_up, ffn_conv, ffn_down, final_g):
    bsz, seq, d = x.shape
    n_layers = ada_w.shape[0]
    assert (seq, d) == (SEQ, D_MODEL)
    x2 = x.reshape(bsz * seq, d)

    mod = _ada(c, ada_w, ada_b)

    for l in range(n_layers):
        mod3 = mod[l].reshape(bsz, 1, 6 * d)
        w_in_b = w_in[l].astype(BF16)
        w_branch_b = w_branch[l].astype(BF16)
        w_out_b = w_out[l].astype(BF16)
        up_b = jnp.concatenate([_pad_cols(ffn_up[l][:, :FFN_HIDDEN], FFN_PAD),
                                _pad_cols(ffn_up[l][:, FFN_HIDDEN:], FFN_PAD)], axis=1).astype(BF16)
        conv_p = jnp.concatenate([_pad_cols(ffn_conv[l][:, :FFN_HIDDEN], FFN_PAD),
                                  _pad_cols(ffn_conv[l][:, FFN_HIDDEN:], FFN_PAD)], axis=1)
        down_b = jnp.pad(ffn_down[l], ((0, FFN_PAD - FFN_HIDDEN), (0, 0))).astype(BF16)

        h = _norm_mod(x2, norm_mix_g[l], mod3, sc_idx=1, sh_idx=0)
        p = _mix_proj(h, w_in_b)
        y = _mixers(p, pool_w[l].astype(BF16), pool_scale[l].reshape(1, -1),
                    sgu_ln_g[l].reshape(1, -1), sgu_ln_b[l].reshape(1, -1), sgu_w[l],
                    sgu_b[l][:, :, None], sconv_w[l], conf_dw_w[l],
                    conf_dw_b[l].reshape(1, -1), conf_ln_g[l].reshape(1, -1),
                    conf_ln_b[l].reshape(1, -1))
        merged = _gate_lift(h, y, w_in_b, w_branch_b, gate_b[l])
        x2 = _resid_mm(merged, w_out_b, x2, mod3, g_idx=2, tm=1024, tn=512, tk=d)

        h = _norm_mod(x2, norm_ffn_g[l], mod3, sc_idx=4, sh_idx=3)
        act = _ffn_up(h, up_b, conv_p)
        x2 = _resid_mm(act, down_b, x2, mod3, g_idx=5, tm=1024, tn=1024, tk=FFN_PAD // 4)

    return _final_norm(x2, final_g).reshape(bsz, seq, d)
```

```python
import functools

import jax
import jax.numpy as jnp
from jax import lax
from jax.experimental import pallas as pl
from jax.experimental.pallas import tpu as pltpu

F32 = jnp.float32
BF16 = jnp.bfloat16

D_MODEL = 4096
SEQ = 2048
W_MIX = 1024
N_MIX_IN = 8 * W_MIX
POOL_WINDOWS = (2, 4, 8, 16)
POOL_GROUP = W_MIX // len(POOL_WINDOWS)
CHUNK = 128
SGU_HEADS = W_MIX // CHUNK
SHORT_K = 3
CONF_K = 31
N_BRANCH = 4
FFN_HIDDEN = 11008
FFN_CONV_K = 3
RMS_EPS = 1e-6
LN_EPS = 1e-5

V7X_VMEM_LIMIT = 60000 * 1024
SUBLANES = 8
MXU_COLS = 256

TM = 1024
TN_MIX_PROJ = 512
TN_GATE = 512
TN_OUT = 512
FFN_TN = 512
FFN_PAD = 11264
TN_DOWN = 1024
TK_DOWN = FFN_PAD // 4
T_MIXER = 256
T_NORM = 256
FFN_SRC_BLOCKS = FFN_HIDDEN // MXU_COLS
FFN_PAD_BLOCKS = FFN_PAD // MXU_COLS


def _cparams(sem):
    return pltpu.CompilerParams(dimension_semantics=sem, vmem_limit_bytes=V7X_VMEM_LIMIT)


def _cast_kernel(x_ref, o_ref):
    o_ref[...] = x_ref[...].astype(o_ref.dtype)


def _cast_bf16(w, tr, tc):
    n_layers, r, c = w.shape
    return pl.pallas_call(
        _cast_kernel,
        grid=(n_layers, r // tr, c // tc),
        in_specs=[pl.BlockSpec((None, tr, tc), lambda l, i, j: (l, i, j))],
        out_specs=pl.BlockSpec((None, tr, tc), lambda l, i, j: (l, i, j)),
        out_shape=jax.ShapeDtypeStruct(w.shape, BF16),
        compiler_params=_cparams(("arbitrary",) * 3),
        name="cast_bf16",
    )(w)


def _ffn_up_src(jo):
    per_half = FFN_TN // MXU_COLS
    step, within = jo // (2 * per_half), jo % (2 * per_half)
    half, sub = within // per_half, within % per_half
    c = step * per_half + sub
    return half * FFN_SRC_BLOCKS + jnp.minimum(c, FFN_SRC_BLOCKS - 1), c >= FFN_SRC_BLOCKS


def _cast_ffn_up_kernel(x_ref, o_ref):
    _, is_pad = _ffn_up_src(pl.program_id(1))
    v = x_ref[...].astype(o_ref.dtype)
    o_ref[...] = jnp.where(is_pad, jnp.zeros_like(v), v)


def _cast_ffn_up(w):
    n_layers, d, _ = w.shape
    return pl.pallas_call(
        _cast_ffn_up_kernel,
        grid=(n_layers, 2 * FFN_PAD_BLOCKS),
        in_specs=[pl.BlockSpec((None, d, MXU_COLS), lambda l, jo: (l, 0, _ffn_up_src(jo)[0]))],
        out_specs=pl.BlockSpec((None, d, MXU_COLS), lambda l, jo: (l, 0, jo)),
        out_shape=jax.ShapeDtypeStruct((n_layers, d, 2 * FFN_PAD), BF16),
        compiler_params=_cparams(("arbitrary",) * 2),
        name="cast_ffn_up",
    )(w)


def _cast_ffn_down_kernel(x_ref, o_ref):
    v = x_ref[...].astype(o_ref.dtype)
    o_ref[...] = jnp.where(pl.program_id(1) >= FFN_SRC_BLOCKS, jnp.zeros_like(v), v)


def _cast_ffn_down(w):
    n_layers, _, d = w.shape
    return pl.pallas_call(
        _cast_ffn_down_kernel,
        grid=(n_layers, FFN_PAD_BLOCKS),
        in_specs=[pl.BlockSpec((None, MXU_COLS, d),
                               lambda l, i: (l, jnp.minimum(i, FFN_SRC_BLOCKS - 1), 0))],
        out_specs=pl.BlockSpec((None, MXU_COLS, d), lambda l, i: (l, i, 0)),
        out_shape=jax.ShapeDtypeStruct((n_layers, FFN_PAD, d), BF16),
        compiler_params=_cparams(("arbitrary",) * 2),
        name="cast_ffn_down",
    )(w)


def _reblock_ffn_conv(conv):
    n_layers, k, _ = conv.shape
    halves = conv.reshape(n_layers, k, 2, FFN_HIDDEN)
    halves = jnp.pad(halves, ((0, 0), (0, 0), (0, 0), (0, FFN_PAD - FFN_HIDDEN)))
    halves = halves.reshape(n_layers, k, 2, FFN_PAD // FFN_TN, FFN_TN)
    return jnp.swapaxes(halves, 2, 3).reshape(n_layers, k, 2 * FFN_PAD)


def _ada_kernel(c_ref, w_ref, b_ref, o_ref):
    cond = jax.nn.silu(c_ref[...]).astype(BF16)
    o_ref[...] = jnp.dot(cond, w_ref[...].astype(BF16), preferred_element_type=F32) + b_ref[...]


def _ada(c, ada_w, ada_b, tn=1024):
    n_layers, d, n = ada_w.shape
    bsz = c.shape[0]
    return pl.pallas_call(
        _ada_kernel,
        grid=(n_layers, n // tn),
        in_specs=[
            pl.BlockSpec((bsz, d), lambda l, j: (0, 0)),
            pl.BlockSpec((None, d, tn), lambda l, j: (l, 0, j)),
            pl.BlockSpec((None, 1, tn), lambda l, j: (l, 0, j)),
        ],
        out_specs=pl.BlockSpec((None, bsz, tn), lambda l, j: (l, 0, j)),
        out_shape=jax.ShapeDtypeStruct((n_layers, bsz, n), F32),
        compiler_params=_cparams(("arbitrary", "arbitrary")),
        name="ada_mod",
    )(c, ada_w, ada_b.reshape(n_layers, 1, n))


def _norm_mod_kernel(x_ref, g_ref, sc_ref, sh_ref, o_ref):
    x = x_ref[...]
    y = x * lax.rsqrt(jnp.mean(x * x, axis=-1, keepdims=True) + RMS_EPS) * g_ref[...]
    o_ref[...] = (y * (1.0 + sc_ref[...]) + sh_ref[...]).astype(o_ref.dtype)


def _norm_kernel(x_ref, g_ref, o_ref):
    x = x_ref[...]
    y = x * lax.rsqrt(jnp.mean(x * x, axis=-1, keepdims=True) + RMS_EPS) * g_ref[...]
    o_ref[...] = y.astype(o_ref.dtype)


def _norm_mod(x2, g, mod3, sc_idx, sh_idx, tr=T_NORM):
    t, d = x2.shape
    per_seq = SEQ // tr
    return pl.pallas_call(
        _norm_mod_kernel,
        grid=(t // tr,),
        in_specs=[
            pl.BlockSpec((tr, d), lambda i: (i, 0)),
            pl.BlockSpec((1, d), lambda i: (0, 0)),
            pl.BlockSpec((None, 1, d), lambda i: (i // per_seq, 0, sc_idx)),
            pl.BlockSpec((None, 1, d), lambda i: (i // per_seq, 0, sh_idx)),
        ],
        out_specs=pl.BlockSpec((tr, d), lambda i: (i, 0)),
        out_shape=jax.ShapeDtypeStruct((t, d), BF16),
        compiler_params=_cparams(("arbitrary",)),
        name="norm_mod",
    )(x2, g.reshape(1, d), mod3, mod3)


def _final_norm(x2, g, tr=T_NORM):
    t, d = x2.shape
    return pl.pallas_call(
        _norm_kernel,
        grid=(t // tr,),
        in_specs=[
            pl.BlockSpec((tr, d), lambda i: (i, 0)),
            pl.BlockSpec((1, d), lambda i: (0, 0)),
        ],
        out_specs=pl.BlockSpec((tr, d), lambda i: (i, 0)),
        out_shape=jax.ShapeDtypeStruct((t, d), F32),
        compiler_params=_cparams(("arbitrary",)),
        name="final_norm",
    )(x2, g.reshape(1, d))


def _mm_kernel(a_ref, b_ref, o_ref):
    o_ref[...] = jnp.dot(a_ref[...], b_ref[...], preferred_element_type=F32).astype(o_ref.dtype)


def _mix_proj(h, w_in_b, l, tm=TM, tn=TN_MIX_PROJ):
    t, d = h.shape
    return pl.pallas_call(
        _mm_kernel,
        grid=(t // tm, N_MIX_IN // tn),
        in_specs=[
            pl.BlockSpec((tm, d), lambda i, j: (i, 0)),
            pl.BlockSpec((None, d, tn), lambda i, j: (l, 0, j)),
        ],
        out_specs=pl.BlockSpec((tm, tn), lambda i, j: (i, j)),
        out_shape=jax.ShapeDtypeStruct((t, N_MIX_IN), F32),
        compiler_params=_cparams(("arbitrary", "arbitrary")),
        name="mix_proj",
    )(h, w_in_b)


A_HALO = 16
Q_HALO = 8
D_HALO = 32
CONV_ROWS = 32


def _carry_halo(buf, halo, tt, first):
    @pl.when(first)
    def _():
        buf[0:halo, :] = jnp.zeros((halo, buf.shape[1]), buf.dtype)

    @pl.when(jnp.logical_not(first))
    def _():
        buf[0:halo, :] = buf[tt:tt + halo, :]


def _gelu_exact(x):
    return 0.5 * x * (1.0 + lax.erf(x * (2.0 ** -0.5)))


def _layer_norm(v, g, b):
    mu = jnp.mean(v, axis=-1, keepdims=True)
    vc = v - mu
    var = jnp.mean(vc * vc, axis=-1, keepdims=True)
    return vc * lax.rsqrt(var + LN_EPS) * g + b


def _mixer_kernel(p_ref, poolw_ref, pools_ref, slng_ref, slnb_ref, sw_ref, sbt_ref, scw_ref,
                  dww_ref, dwb_ref, clng_ref, clnb_ref, y_ref, abuf, qbuf, dbuf, sbuf):
    tt = p_ref.shape[0]
    per_seq = SEQ // tt
    i = pl.program_id(0)
    first = (i % per_seq) == 0
    t0 = (i % per_seq) * tt

    _carry_halo(abuf, A_HALO, tt, first)
    abuf[A_HALO:A_HALO + tt, :] = p_ref[:, 0:W_MIX]
    pos = (lax.broadcasted_iota(jnp.int32, (tt, 1), 0) + t0 + 1).astype(F32)
    for g, w in enumerate(POOL_WINDOWS):
        c0 = g * POOL_GROUP
        cur = abuf[A_HALO:A_HALO + tt, c0:c0 + POOL_GROUP]
        win = cur
        for j in range(1, w):
            win = win + abuf[A_HALO - j:A_HALO - j + tt, c0:c0 + POOL_GROUP]
        pooled = win / jnp.minimum(pos, float(w)) - cur
        ya = jnp.dot(pooled.astype(BF16), poolw_ref[g], preferred_element_type=F32)
        y_ref[:, c0:c0 + POOL_GROUP] = (ya * pools_ref[:, c0:c0 + POOL_GROUP]).astype(y_ref.dtype)

    row = lax.broadcasted_iota(jnp.int32, (CHUNK, CHUNK), 0)
    col = lax.broadcasted_iota(jnp.int32, (CHUNK, CHUNK), 1)
    tril = row >= col
    for c in range(tt // CHUNK):
        r0 = c * CHUNK
        z = _gelu_exact(p_ref[r0:r0 + CHUNK, W_MIX:3 * W_MIX])
        u = z[:, :W_MIX]
        v = _layer_norm(z[:, W_MIX:], slng_ref[...], slnb_ref[...]).astype(BF16)
        for hd in range(SGU_HEADS):
            h0 = hd * CHUNK
            wm = jnp.where(tril, sw_ref[hd], 0.0).astype(BF16)
            sv = jnp.dot(wm, v[:, h0:h0 + CHUNK], preferred_element_type=F32) + sbt_ref[hd]
            y_ref[r0:r0 + CHUNK, W_MIX + h0:W_MIX + h0 + CHUNK] = (
                u[:, h0:h0 + CHUNK] * sv).astype(y_ref.dtype)

    _carry_halo(qbuf, Q_HALO, tt, first)
    q = p_ref[:, 4 * W_MIX:5 * W_MIX] * p_ref[:, 5 * W_MIX:6 * W_MIX]
    qbuf[Q_HALO:Q_HALO + tt, :] = q
    conv = scw_ref[SHORT_K - 1:SHORT_K, :] * q
    for k in range(SHORT_K - 1):
        s = SHORT_K - 1 - k
        conv = conv + scw_ref[k:k + 1, :] * qbuf[Q_HALO - s:Q_HALO - s + tt, :]
    y_ref[:, 2 * W_MIX:3 * W_MIX] = (p_ref[:, 3 * W_MIX:4 * W_MIX] * conv).astype(y_ref.dtype)

    _carry_halo(dbuf, D_HALO, tt, first)
    dbuf[D_HALO:D_HALO + tt, :] = (p_ref[:, 6 * W_MIX:7 * W_MIX]
                                   * jax.nn.sigmoid(p_ref[:, 7 * W_MIX:8 * W_MIX]))
    n_s = tt + D_HALO
    for r in range(SUBLANES):
        sbuf[r, 0:n_s - r, :] = dbuf[r:n_s, :]

    def conv_rows(ci, carry):
        r0 = pl.multiple_of(ci * CONV_ROWS, CONV_ROWS)
        acc = jnp.broadcast_to(dwb_ref[...], (CONV_ROWS, W_MIX))
        for k in range(CONF_K):
            qq, rr = divmod(k + D_HALO - (CONF_K - 1), SUBLANES)
            acc = acc + dww_ref[k:k + 1, :] * sbuf[rr, pl.ds(r0 + qq * SUBLANES, CONV_ROWS), :]
        yd = _layer_norm(acc, clng_ref[...], clnb_ref[...])
        y_ref[pl.ds(r0, CONV_ROWS), 3 * W_MIX:4 * W_MIX] = jax.nn.silu(yd).astype(y_ref.dtype)
        return carry

    lax.fori_loop(0, tt // CONV_ROWS, conv_rows, 0)


def _mixers(p, pool_w_b, pool_scale, sgu_ln_g, sgu_ln_b, sgu_w, sgu_b_t, sconv_w,
            conf_dw_w, conf_dw_b, conf_ln_g, conf_ln_b, tt=T_MIXER):
    t = p.shape[0]

    def full(a):
        nd = a.ndim
        return pl.BlockSpec(a.shape, lambda i: (0,) * nd)

    small = (pool_w_b, pool_scale, sgu_ln_g, sgu_ln_b, sgu_w, sgu_b_t, sconv_w,
             conf_dw_w, conf_dw_b, conf_ln_g, conf_ln_b)
    return pl.pallas_call(
        _mixer_kernel,
        grid=(t // tt,),
        in_specs=[pl.BlockSpec((tt, N_MIX_IN), lambda i: (i, 0))] + [full(a) for a in small],
        out_specs=pl.BlockSpec((tt, N_BRANCH * W_MIX), lambda i: (i, 0)),
        out_shape=jax.ShapeDtypeStruct((t, N_BRANCH * W_MIX), BF16),
        scratch_shapes=[
            pltpu.VMEM((A_HALO + tt, W_MIX), F32),
            pltpu.VMEM((Q_HALO + tt, W_MIX), F32),
            pltpu.VMEM((D_HALO + tt, W_MIX), F32),
            pltpu.VMEM((SUBLANES, D_HALO + tt, W_MIX), F32),
        ],
        compiler_params=_cparams(("arbitrary",)),
        name="mixers",
    )(p, *small)


def _gate_lift_kernel(h_ref, y_ref, wg_ref, wb_ref, gb_ref, o_ref, acc_ref):
    br = pl.program_id(2)

    @pl.when(br == 0)
    def _():
        acc_ref[...] = jnp.zeros(acc_ref.shape, F32)

    gate = jnp.dot(h_ref[...], wg_ref[...], preferred_element_type=F32) + gb_ref[...]
    lift = jnp.dot(y_ref[...], wb_ref[...], preferred_element_type=F32)
    acc_ref[...] += jax.nn.sigmoid(gate) * lift

    @pl.when(br == N_BRANCH - 1)
    def _():
        o_ref[...] = acc_ref[...].astype(o_ref.dtype)


def _gate_lift(h, y, w_in_b, w_branch_b, gate_b, l, tm=TM, tn=TN_GATE):
    t, d = h.shape
    nj = d // tn
    gate_col0 = N_MIX_IN // tn
    return pl.pallas_call(
        _gate_lift_kernel,
        grid=(t // tm, nj, N_BRANCH),
        in_specs=[
            pl.BlockSpec((tm, d), lambda i, j, b: (i, 0)),
            pl.BlockSpec((tm, W_MIX), lambda i, j, b: (i, b)),
            pl.BlockSpec((None, d, tn), lambda i, j, b: (l, 0, gate_col0 + b * nj + j)),
            pl.BlockSpec((None, W_MIX, tn), lambda i, j, b: (l, b, j)),
            pl.BlockSpec((None, 1, tn), lambda i, j, b: (l, 0, b * nj + j)),
        ],
        out_specs=pl.BlockSpec((tm, tn), lambda i, j, b: (i, j)),
        out_shape=jax.ShapeDtypeStruct((t, d), BF16),
        scratch_shapes=[pltpu.VMEM((tm, tn), F32)],
        compiler_params=_cparams(("arbitrary", "arbitrary", "arbitrary")),
        name="gate_lift",
    )(h, y, w_in_b, w_branch_b, gate_b.reshape(gate_b.shape[0], 1, -1))


def _resid_mm_kernel(a_ref, w_ref, x_ref, g_ref, o_ref, acc_ref, *, nk):
    if nk == 1:
        part = jnp.dot(a_ref[...], w_ref[...], preferred_element_type=F32)
        o_ref[...] = x_ref[...] + g_ref[...] * part
        return

    k = pl.program_id(2)

    @pl.when(k == 0)
    def _():
        acc_ref[...] = jnp.zeros(acc_ref.shape, F32)

    acc_ref[...] += jnp.dot(a_ref[...], w_ref[...], preferred_element_type=F32)

    @pl.when(k == nk - 1)
    def _():
        o_ref[...] = x_ref[...] + g_ref[...] * acc_ref[...]


def _resid_mm(a, w_b, x2, mod3, l, g_idx, tm, tn, tk):
    t, kdim = a.shape
    d = w_b.shape[2]
    nk = kdim // tk
    per_seq = SEQ // tm
    nj = d // tn
    return pl.pallas_call(
        functools.partial(_resid_mm_kernel, nk=nk),
        grid=(t // tm, nj, nk),
        in_specs=[
            pl.BlockSpec((tm, tk), lambda i, j, k: (i, k)),
            pl.BlockSpec((None, tk, tn), lambda i, j, k: (l, k, j)),
            pl.BlockSpec((tm, tn), lambda i, j, k: (i, j)),
            pl.BlockSpec((None, 1, tn), lambda i, j, k: (i // per_seq, 0, g_idx * nj + j)),
        ],
        out_specs=pl.BlockSpec((tm, tn), lambda i, j, k: (i, j)),
        out_shape=jax.ShapeDtypeStruct((t, d), F32),
        scratch_shapes=[pltpu.VMEM((tm, tn), F32)],
        compiler_params=_cparams(("arbitrary", "arbitrary", "arbitrary")),
        name="resid_mm",
    )(a, w_b, x2, mod3)


Z_HALO = 8


def _ffn_up_kernel(h_ref, w_ref, c_ref, o_ref, zbuf):
    tm = h_ref.shape[0]
    tn = o_ref.shape[1]
    per_seq = SEQ // tm

    @pl.when((pl.program_id(1) % per_seq) == 0)
    def _():
        zbuf[tm:tm + Z_HALO, :] = jnp.zeros((Z_HALO, zbuf.shape[1]), F32)

    z = jnp.dot(h_ref[...], w_ref[...], preferred_element_type=F32)
    zbuf[0:Z_HALO, :] = zbuf[tm:tm + Z_HALO, :]
    zbuf[Z_HALO:Z_HALO + tm, :] = z
    conv = c_ref[FFN_CONV_K - 1:FFN_CONV_K, :] * z
    for k in range(FFN_CONV_K - 1):
        s = FFN_CONV_K - 1 - k
        conv = conv + c_ref[k:k + 1, :] * zbuf[Z_HALO - s:Z_HALO - s + tm, :]
    o_ref[...] = (jax.nn.silu(conv[:, :tn]) * conv[:, tn:]).astype(o_ref.dtype)


def _ffn_up(h, up_b, conv_p, l, tm=TM, tn=FFN_TN):
    t, d = h.shape
    return pl.pallas_call(
        _ffn_up_kernel,
        grid=(FFN_PAD // tn, t // tm),
        in_specs=[
            pl.BlockSpec((tm, d), lambda j, i: (i, 0)),
            pl.BlockSpec((None, d, 2 * tn), lambda j, i: (l, 0, j)),
            pl.BlockSpec((None, FFN_CONV_K, 2 * tn), lambda j, i: (l, 0, j)),
        ],
        out_specs=pl.BlockSpec((tm, tn), lambda j, i: (i, j)),
        out_shape=jax.ShapeDtypeStruct((t, FFN_PAD), BF16),
        scratch_shapes=[pltpu.VMEM((Z_HALO + tm, 2 * tn), F32)],
        compiler_params=_cparams(("arbitrary", "arbitrary")),
        name="ffn_up",
    )(h, up_b, conv_p)


def kernel(x, c, ada_w, ada_b, norm_mix_g, w_in, pool_w, pool_scale, sgu_ln_g, sgu_ln_b, sgu_w,
           sgu_b, sconv_w, conf_dw_w, conf_dw_b, conf_ln_g, conf_ln_b, gate_b, w_branch, w_out,
           norm_ffn_g, ffn_up, ffn_conv, ffn_down, final_g):
    bsz, seq, d = x.shape
    n_layers = ada_w.shape[0]
    assert (seq, d) == (SEQ, D_MODEL)
    x2 = x.reshape(bsz * seq, d)

    mod = _ada(c, ada_w, ada_b)
    w_in_b = _cast_bf16(w_in, 512, 4096)
    w_branch_b = _cast_bf16(w_branch, 512, 4096)
    w_out_b = _cast_bf16(w_out, 512, 4096)
    up_b = _cast_ffn_up(ffn_up)
    down_b = _cast_ffn_down(ffn_down)
    conv_p = _reblock_ffn_conv(ffn_conv)

    for l in range(n_layers):
        mod3 = mod[l].reshape(bsz, 1, 6 * d)

        h = _norm_mod(x2, norm_mix_g[l], mod3, sc_idx=1, sh_idx=0)
        p = _mix_proj(h, w_in_b, l)
        y = _mixers(p, pool_w[l].astype(BF16), pool_scale[l].reshape(1, -1),
                    sgu_ln_g[l].reshape(1, -1), sgu_ln_b[l].reshape(1, -1), sgu_w[l],
                    sgu_b[l][:, :, None], sconv_w[l], conf_dw_w[l],
                    conf_dw_b[l].reshape(1, -1), conf_ln_g[l].reshape(1, -1),
                    conf_ln_b[l].reshape(1, -1))
        merged = _gate_lift(h, y, w_in_b, w_branch_b, gate_b, l)
        x2 = _resid_mm(merged, w_out_b, x2, mod3, l, g_idx=2, tm=TM, tn=TN_OUT, tk=d)

        h = _norm_mod(x2, norm_ffn_g[l], mod3, sc_idx=4, sh_idx=3)
        act = _ffn_up(h, up_b, conv_p, l)
        x2 = _resid_mm(act, down_b, x2, mod3, l, g_idx=5, tm=TM, tn=TN_DOWN, tk=TK_DOWN)

    return _final_norm(x2, final_g).reshape(bsz, seq, d)
```

```python
import functools

import jax
import jax.numpy as jnp
from jax import lax
from jax.experimental import pallas as pl
from jax.experimental.pallas import tpu as pltpu

F32 = jnp.float32
BF16 = jnp.bfloat16

D_MODEL = 4096
SEQ = 2048
W_MIX = 1024
N_MIX_IN = 8 * W_MIX
POOL_WINDOWS = (2, 4, 8, 16)
POOL_GROUP = W_MIX // len(POOL_WINDOWS)
CHUNK = 128
SGU_HEADS = W_MIX // CHUNK
SHORT_K = 3
CONF_K = 31
N_BRANCH = 4
FFN_HIDDEN = 11008
FFN_CONV_K = 3
RMS_EPS = 1e-6
LN_EPS = 1e-5

V7X_VMEM_LIMIT = 60000 * 1024
SUBLANES = 8
MXU_COLS = 256

TM = 1024
TN_MIX_PROJ = 1024
TN_GATE = 1024
TN_OUT = 1024
FFN_TN = 512
FFN_PAD = 11264
TN_DOWN = 1024
TK_DOWN = FFN_PAD // 4
T_MIXER = 256
T_NORM = 512
FFN_SRC_BLOCKS = FFN_HIDDEN // MXU_COLS
FFN_PAD_BLOCKS = FFN_PAD // MXU_COLS


def _cparams(sem):
    return pltpu.CompilerParams(dimension_semantics=sem, vmem_limit_bytes=V7X_VMEM_LIMIT)


def _cast_kernel(x_ref, o_ref):
    o_ref[...] = x_ref[...].astype(o_ref.dtype)


def _cast_bf16(w, tr, tc):
    n_layers, r, c = w.shape
    return pl.pallas_call(
        _cast_kernel,
        grid=(n_layers, r // tr, c // tc),
        in_specs=[pl.BlockSpec((None, tr, tc), lambda l, i, j: (l, i, j))],
        out_specs=pl.BlockSpec((None, tr, tc), lambda l, i, j: (l, i, j)),
        out_shape=jax.ShapeDtypeStruct(w.shape, BF16),
        compiler_params=_cparams(("arbitrary",) * 3),
        name="cast_bf16",
    )(w)


def _ffn_up_src(jo):
    per_half = FFN_TN // MXU_COLS
    step, within = jo // (2 * per_half), jo % (2 * per_half)
    half, sub = within // per_half, within % per_half
    c = step * per_half + sub
    return half * FFN_SRC_BLOCKS + jnp.minimum(c, FFN_SRC_BLOCKS - 1), c >= FFN_SRC_BLOCKS


def _cast_ffn_up_kernel(x_ref, o_ref):
    _, is_pad = _ffn_up_src(pl.program_id(1))
    v = x_ref[...].astype(o_ref.dtype)
    o_ref[...] = jnp.where(is_pad, jnp.zeros_like(v), v)


def _cast_ffn_up(w):
    n_layers, d, _ = w.shape
    return pl.pallas_call(
        _cast_ffn_up_kernel,
        grid=(n_layers, 2 * FFN_PAD_BLOCKS),
        in_specs=[pl.BlockSpec((None, d, MXU_COLS), lambda l, jo: (l, 0, _ffn_up_src(jo)[0]))],
        out_specs=pl.BlockSpec((None, d, MXU_COLS), lambda l, jo: (l, 0, jo)),
        out_shape=jax.ShapeDtypeStruct((n_layers, d, 2 * FFN_PAD), BF16),
        compiler_params=_cparams(("arbitrary",) * 2),
        name="cast_ffn_up",
    )(w)


def _cast_ffn_down_kernel(x_ref, o_ref):
    v = x_ref[...].astype(o_ref.dtype)
    o_ref[...] = jnp.where(pl.program_id(1) >= FFN_SRC_BLOCKS, jnp.zeros_like(v), v)


def _cast_ffn_down(w):
    n_layers, _, d = w.shape
    return pl.pallas_call(
        _cast_ffn_down_kernel,
        grid=(n_layers, FFN_PAD_BLOCKS),
        in_specs=[pl.BlockSpec((None, MXU_COLS, d),
                               lambda l, i: (l, jnp.minimum(i, FFN_SRC_BLOCKS - 1), 0))],
        out_specs=pl.BlockSpec((None, MXU_COLS, d), lambda l, i: (l, i, 0)),
        out_shape=jax.ShapeDtypeStruct((n_layers, FFN_PAD, d), BF16),
        compiler_params=_cparams(("arbitrary",) * 2),
        name="cast_ffn_down",
    )(w)


def _reblock_ffn_conv(conv):
    n_layers, k, _ = conv.shape
    halves = conv.reshape(n_layers, k, 2, FFN_HIDDEN)
    halves = jnp.pad(halves, ((0, 0), (0, 0), (0, 0), (0, FFN_PAD - FFN_HIDDEN)))
    halves = halves.reshape(n_layers, k, 2, FFN_PAD // FFN_TN, FFN_TN)
    return jnp.swapaxes(halves, 2, 3).reshape(n_layers, k, 2 * FFN_PAD)


def _ada_kernel(c_ref, w_ref, b_ref, o_ref):
    cond = jax.nn.silu(c_ref[...]).astype(BF16)
    o_ref[...] = jnp.dot(cond, w_ref[...].astype(BF16), preferred_element_type=F32) + b_ref[...]


def _ada(c, ada_w, ada_b, tn=1024):
    n_layers, d, n = ada_w.shape
    bsz = c.shape[0]
    return pl.pallas_call(
        _ada_kernel,
        grid=(n_layers, n // tn),
        in_specs=[
            pl.BlockSpec((bsz, d), lambda l, j: (0, 0)),
            pl.BlockSpec((None, d, tn), lambda l, j: (l, 0, j)),
            pl.BlockSpec((None, 1, tn), lambda l, j: (l, 0, j)),
        ],
        out_specs=pl.BlockSpec((None, bsz, tn), lambda l, j: (l, 0, j)),
        out_shape=jax.ShapeDtypeStruct((n_layers, bsz, n), F32),
        compiler_params=_cparams(("arbitrary", "arbitrary")),
        name="ada_mod",
    )(c, ada_w, ada_b.reshape(n_layers, 1, n))


def _norm_mod_kernel(x_ref, g_ref, sc_ref, sh_ref, o_ref):
    x = x_ref[...]
    y = x * lax.rsqrt(jnp.mean(x * x, axis=-1, keepdims=True) + RMS_EPS) * g_ref[...]
    o_ref[...] = (y * (1.0 + sc_ref[...]) + sh_ref[...]).astype(o_ref.dtype)


def _norm_kernel(x_ref, g_ref, o_ref):
    x = x_ref[...]
    y = x * lax.rsqrt(jnp.mean(x * x, axis=-1, keepdims=True) + RMS_EPS) * g_ref[...]
    o_ref[...] = y.astype(o_ref.dtype)


def _norm_mod(x2, g, mod3, sc_idx, sh_idx, tr=T_NORM):
    t, d = x2.shape
    per_seq = SEQ // tr
    return pl.pallas_call(
        _norm_mod_kernel,
        grid=(t // tr,),
        in_specs=[
            pl.BlockSpec((tr, d), lambda i: (i, 0)),
            pl.BlockSpec((1, d), lambda i: (0, 0)),
            pl.BlockSpec((None, 1, d), lambda i: (i // per_seq, 0, sc_idx)),
            pl.BlockSpec((None, 1, d), lambda i: (i // per_seq, 0, sh_idx)),
        ],
        out_specs=pl.BlockSpec((tr, d), lambda i: (i, 0)),
        out_shape=jax.ShapeDtypeStruct((t, d), BF16),
        compiler_params=_cparams(("arbitrary",)),
        name="norm_mod",
    )(x2, g.reshape(1, d), mod3, mod3)


def _final_norm(x2, g, tr=T_NORM):
    t, d = x2.shape
    return pl.pallas_call(
        _norm_kernel,
        grid=(t // tr,),
        in_specs=[
            pl.BlockSpec((tr, d), lambda i: (i, 0)),
            pl.BlockSpec((1, d), lambda i: (0, 0)),
        ],
        out_specs=pl.BlockSpec((tr, d), lambda i: (i, 0)),
        out_shape=jax.ShapeDtypeStruct((t, d), F32),
        compiler_params=_cparams(("arbitrary",)),
        name="final_norm",
    )(x2, g.reshape(1, d))


def _mm_kernel(a_ref, b_ref, o_ref):
    o_ref[...] = jnp.dot(a_ref[...], b_ref[...], preferred_element_type=F32).astype(o_ref.dtype)


def _mix_proj(h, w_in_b, l, tm=TM, tn=TN_MIX_PROJ):
    t, d = h.shape
    return pl.pallas_call(
        _mm_kernel,
        grid=(t // tm, N_MIX_IN // tn),
        in_specs=[
            pl.BlockSpec((tm, d), lambda i, j: (i, 0)),
            pl.BlockSpec((None, d, tn), lambda i, j: (l, 0, j)),
        ],
        out_specs=pl.BlockSpec((tm, tn), lambda i, j: (i, j)),
        out_shape=jax.ShapeDtypeStruct((t, N_MIX_IN), F32),
        compiler_params=_cparams(("arbitrary", "arbitrary")),
        name="mix_proj",
    )(h, w_in_b)


A_HALO = 16
Q_HALO = 8
D_HALO = 32
CONV_ROWS = 16


def _carry_halo(buf, halo, tt, first):
    @pl.when(first)
    def _():
        buf[0:halo, :] = jnp.zeros((halo, buf.shape[1]), buf.dtype)

    @pl.when(jnp.logical_not(first))
    def _():
        buf[0:halo, :] = buf[tt:tt + halo, :]


def _gelu_exact(x):
    return 0.5 * x * (1.0 + lax.erf(x * (2.0 ** -0.5)))


def _layer_norm(v, g, b):
    mu = jnp.mean(v, axis=-1, keepdims=True)
    vc = v - mu
    var = jnp.mean(vc * vc, axis=-1, keepdims=True)
    return vc * lax.rsqrt(var + LN_EPS) * g + b


def _mixer_kernel(p_ref, poolw_ref, pools_ref, slng_ref, slnb_ref, sw_ref, sbt_ref, scw_ref,
                  dww_ref, dwb_ref, clng_ref, clnb_ref, y_ref, abuf, qbuf, dbuf, sbuf, cbuf):
    tt = p_ref.shape[0]
    per_seq = SEQ // tt
    i = pl.program_id(0)
    first = (i % per_seq) == 0
    t0 = (i % per_seq) * tt

    _carry_halo(abuf, A_HALO, tt, first)
    abuf[A_HALO:A_HALO + tt, :] = p_ref[:, 0:W_MIX]
    pos = (lax.broadcasted_iota(jnp.int32, (tt, 1), 0) + t0 + 1).astype(F32)
    for g, w in enumerate(POOL_WINDOWS):
        c0 = g * POOL_GROUP
        cur = abuf[A_HALO:A_HALO + tt, c0:c0 + POOL_GROUP]
        win = cur
        for j in range(1, w):
            win = win + abuf[A_HALO - j:A_HALO - j + tt, c0:c0 + POOL_GROUP]
        pooled = win / jnp.minimum(pos, float(w)) - cur
        ya = jnp.dot(pooled.astype(BF16), poolw_ref[g], preferred_element_type=F32)
        y_ref[:, c0:c0 + POOL_GROUP] = (ya * pools_ref[:, c0:c0 + POOL_GROUP]).astype(y_ref.dtype)

    row = lax.broadcasted_iota(jnp.int32, (CHUNK, CHUNK), 0)
    col = lax.broadcasted_iota(jnp.int32, (CHUNK, CHUNK), 1)
    tril = row >= col
    for c in range(tt // CHUNK):
        r0 = c * CHUNK
        z = _gelu_exact(p_ref[r0:r0 + CHUNK, W_MIX:3 * W_MIX])
        u = z[:, :W_MIX]
        v = _layer_norm(z[:, W_MIX:], slng_ref[...], slnb_ref[...]).astype(BF16)
        for hd in range(SGU_HEADS):
            h0 = hd * CHUNK
            wm = jnp.where(tril, sw_ref[hd], 0.0).astype(BF16)
            sv = jnp.dot(wm, v[:, h0:h0 + CHUNK], preferred_element_type=F32) + sbt_ref[hd]
            y_ref[r0:r0 + CHUNK, W_MIX + h0:W_MIX + h0 + CHUNK] = (
                u[:, h0:h0 + CHUNK] * sv).astype(y_ref.dtype)

    _carry_halo(qbuf, Q_HALO, tt, first)
    q = p_ref[:, 4 * W_MIX:5 * W_MIX] * p_ref[:, 5 * W_MIX:6 * W_MIX]
    qbuf[Q_HALO:Q_HALO + tt, :] = q
    conv = scw_ref[SHORT_K - 1:SHORT_K, :] * q
    for k in range(SHORT_K - 1):
        s = SHORT_K - 1 - k
        conv = conv + scw_ref[k:k + 1, :] * qbuf[Q_HALO - s:Q_HALO - s + tt, :]
    y_ref[:, 2 * W_MIX:3 * W_MIX] = (p_ref[:, 3 * W_MIX:4 * W_MIX] * conv).astype(y_ref.dtype)

    _carry_halo(dbuf, D_HALO, tt, first)
    dbuf[D_HALO:D_HALO + tt, :] = (p_ref[:, 6 * W_MIX:7 * W_MIX]
                                   * jax.nn.sigmoid(p_ref[:, 7 * W_MIX:8 * W_MIX]))
    n_s = tt + D_HALO
    for r in range(SUBLANES):
        sbuf[r, 0:n_s - r, :] = dbuf[r:n_s, :]

    def conv_rows(ci, carry):
        r0 = pl.multiple_of(ci * CONV_ROWS, CONV_ROWS)
        n_blk = CONV_ROWS // SUBLANES
        accs = [jnp.broadcast_to(dwb_ref[...], (SUBLANES, W_MIX))] * n_blk
        for k in range(CONF_K):
            qq, rr = divmod(k + D_HALO - (CONF_K - 1), SUBLANES)
            wk = dww_ref[k]
            accs = [acc + wk * sbuf[rr, pl.ds(r0 + (qq + b) * SUBLANES, SUBLANES), :]
                    for b, acc in enumerate(accs)]
        for b, acc in enumerate(accs):
            cbuf[pl.ds(r0 + b * SUBLANES, SUBLANES), :] = acc
        return carry

    lax.fori_loop(0, tt // CONV_ROWS, conv_rows, 0)
    yd = _layer_norm(cbuf[...], clng_ref[...], clnb_ref[...])
    y_ref[:, 3 * W_MIX:4 * W_MIX] = jax.nn.silu(yd).astype(y_ref.dtype)


def _mixers(p, pool_w_b, pool_scale, sgu_ln_g, sgu_ln_b, sgu_w, sgu_b_t, sconv_w,
            conf_dw_w, conf_dw_b, conf_ln_g, conf_ln_b, tt=T_MIXER):
    t = p.shape[0]

    def full(a):
        nd = a.ndim
        return pl.BlockSpec(a.shape, lambda i: (0,) * nd)

    small = (pool_w_b, pool_scale, sgu_ln_g, sgu_ln_b, sgu_w, sgu_b_t, sconv_w,
             conf_dw_w, conf_dw_b, conf_ln_g, conf_ln_b)
    return pl.pallas_call(
        _mixer_kernel,
        grid=(t // tt,),
        in_specs=[pl.BlockSpec((tt, N_MIX_IN), lambda i: (i, 0))] + [full(a) for a in small],
        out_specs=pl.BlockSpec((tt, N_BRANCH * W_MIX), lambda i: (i, 0)),
        out_shape=jax.ShapeDtypeStruct((t, N_BRANCH * W_MIX), BF16),
        scratch_shapes=[
            pltpu.VMEM((A_HALO + tt, W_MIX), F32),
            pltpu.VMEM((Q_HALO + tt, W_MIX), F32),
            pltpu.VMEM((D_HALO + tt, W_MIX), F32),
            pltpu.VMEM((SUBLANES, D_HALO + tt, W_MIX), F32),
            pltpu.VMEM((tt, W_MIX), F32),
        ],
        compiler_params=_cparams(("arbitrary",)),
        name="mixers",
    )(p, *small)


def _gate_lift_kernel(h_ref, y_ref, wg_ref, wb_ref, gb_ref, o_ref, acc_ref):
    br = pl.program_id(2)

    @pl.when(br == 0)
    def _():
        acc_ref[...] = jnp.zeros(acc_ref.shape, F32)

    gate = jnp.dot(h_ref[...], wg_ref[...], preferred_element_type=F32) + gb_ref[...]
    lift = jnp.dot(y_ref[...], wb_ref[...], preferred_element_type=F32)
    acc_ref[...] += jax.nn.sigmoid(gate) * lift

    @pl.when(br == N_BRANCH - 1)
    def _():
        o_ref[...] = acc_ref[...].astype(o_ref.dtype)


def _gate_lift(h, y, w_in_b, w_branch_b, gate_b, l, tm=TM, tn=TN_GATE):
    t, d = h.shape
    nj = d // tn
    gate_col0 = N_MIX_IN // tn
    return pl.pallas_call(
        _gate_lift_kernel,
        grid=(t // tm, nj, N_BRANCH),
        in_specs=[
            pl.BlockSpec((tm, d), lambda i, j, b: (i, 0)),
            pl.BlockSpec((tm, W_MIX), lambda i, j, b: (i, b)),
            pl.BlockSpec((None, d, tn), lambda i, j, b: (l, 0, gate_col0 + b * nj + j)),
            pl.BlockSpec((None, W_MIX, tn), lambda i, j, b: (l, b, j)),
            pl.BlockSpec((None, 1, tn), lambda i, j, b: (l, 0, b * nj + j)),
        ],
        out_specs=pl.BlockSpec((tm, tn), lambda i, j, b: (i, j)),
        out_shape=jax.ShapeDtypeStruct((t, d), BF16),
        scratch_shapes=[pltpu.VMEM((tm, tn), F32)],
        compiler_params=_cparams(("arbitrary", "arbitrary", "arbitrary")),
        name="gate_lift",
    )(h, y, w_in_b, w_branch_b, gate_b.reshape(gate_b.shape[0], 1, -1))


def _resid_mm_kernel(a_ref, w_ref, x_ref, g_ref, o_ref, acc_ref, *, nk):
    if nk == 1:
        part = jnp.dot(a_ref[...], w_ref[...], preferred_element_type=F32)
        o_ref[...] = x_ref[...] + g_ref[...] * part
        return

    k = pl.program_id(2)

    @pl.when(k == 0)
    def _():
        acc_ref[...] = jnp.zeros(acc_ref.shape, F32)

    acc_ref[...] += jnp.dot(a_ref[...], w_ref[...], preferred_element_type=F32)

    @pl.when(k == nk - 1)
    def _():
        o_ref[...] = x_ref[...] + g_ref[...] * acc_ref[...]


def _resid_mm(a, w_b, x2, mod3, l, g_idx, tm, tn, tk):
    t, kdim = a.shape
    d = w_b.shape[2]
    nk = kdim // tk
    per_seq = SEQ // tm
    nj = d // tn
    return pl.pallas_call(
        functools.partial(_resid_mm_kernel, nk=nk),
        grid=(t // tm, nj, nk),
        in_specs=[
            pl.BlockSpec((tm, tk), lambda i, j, k: (i, k)),
            pl.BlockSpec((None, tk, tn), lambda i, j, k: (l, k, j)),
            pl.BlockSpec((tm, tn), lambda i, j, k: (i, j)),
            pl.BlockSpec((None, 1, tn), lambda i, j, k: (i // per_seq, 0, g_idx * nj + j)),
        ],
        out_specs=pl.BlockSpec((tm, tn), lambda i, j, k: (i, j)),
        out_shape=jax.ShapeDtypeStruct((t, d), F32),
        scratch_shapes=[pltpu.VMEM((tm, tn) if nk > 1 else (SUBLANES, 128), F32)],
        compiler_params=_cparams(("arbitrary", "arbitrary", "arbitrary")),
        name="resid_mm",
    )(a, w_b, x2, mod3)


Z_HALO = 8


def _ffn_up_kernel(h_ref, w_ref, c_ref, o_ref, zbuf):
    tm = h_ref.shape[0]
    tn = o_ref.shape[1]
    per_seq = SEQ // tm

    @pl.when((pl.program_id(1) % per_seq) == 0)
    def _():
        zbuf[tm:tm + Z_HALO, :] = jnp.zeros((Z_HALO, zbuf.shape[1]), F32)

    z = jnp.dot(h_ref[...], w_ref[...], preferred_element_type=F32)
    zbuf[0:Z_HALO, :] = zbuf[tm:tm + Z_HALO, :]
    zbuf[Z_HALO:Z_HALO + tm, :] = z
    conv = c_ref[FFN_CONV_K - 1:FFN_CONV_K, :] * z
    for k in range(FFN_CONV_K - 1):
        s = FFN_CONV_K - 1 - k
        conv = conv + c_ref[k:k + 1, :] * zbuf[Z_HALO - s:Z_HALO - s + tm, :]
    o_ref[...] = (jax.nn.silu(conv[:, :tn]) * conv[:, tn:]).astype(o_ref.dtype)


def _ffn_up(h, up_b, conv_p, l, tm=TM, tn=FFN_TN):
    t, d = h.shape
    return pl.pallas_call(
        _ffn_up_kernel,
        grid=(FFN_PAD // tn, t // tm),
        in_specs=[
            pl.BlockSpec((tm, d), lambda j, i: (i, 0)),
            pl.BlockSpec((None, d, 2 * tn), lambda j, i: (l, 0, j)),
            pl.BlockSpec((None, FFN_CONV_K, 2 * tn), lambda j, i: (l, 0, j)),
        ],
        out_specs=pl.BlockSpec((tm, tn), lambda j, i: (i, j)),
        out_shape=jax.ShapeDtypeStruct((t, FFN_PAD), BF16),
        scratch_shapes=[pltpu.VMEM((Z_HALO + tm, 2 * tn), F32)],
        compiler_params=_cparams(("arbitrary", "arbitrary")),
        name="ffn_up",
    )(h, up_b, conv_p)


def kernel(x, c, ada_w, ada_b, norm_mix_g, w_in, pool_w, pool_scale, sgu_ln_g, sgu_ln_b, sgu_w,
           sgu_b, sconv_w, conf_dw_w, conf_dw_b, conf_ln_g, conf_ln_b, gate_b, w_branch, w_out,
           norm_ffn_g, ffn_up, ffn_conv, ffn_down, final_g):
    bsz, seq, d = x.shape
    n_layers = ada_w.shape[0]
    assert (seq, d) == (SEQ, D_MODEL)
    x2 = x.reshape(bsz * seq, d)

    mod = _ada(c, ada_w, ada_b)
    w_in_b = _cast_bf16(w_in, 512, 4096)
    w_branch_b = _cast_bf16(w_branch, 512, 4096)
    w_out_b = _cast_bf16(w_out, 512, 4096)
    up_b = _cast_ffn_up(ffn_up)
    down_b = _cast_ffn_down(ffn_down)
    conv_p = _reblock_ffn_conv(ffn_conv)

    for l in range(n_layers):
        mod3 = mod[l].reshape(bsz, 1, 6 * d)

        h = _norm_mod(x2, norm_mix_g[l], mod3, sc_idx=1, sh_idx=0)
        p = _mix_proj(h, w_in_b, l)
        y = _mixers(p, pool_w[l].astype(BF16), pool_scale[l].reshape(1, -1),
                    sgu_ln_g[l].reshape(1, -1), sgu_ln_b[l].reshape(1, -1), sgu_w[l],
                    sgu_b[l][:, :, None], sconv_w[l],
                    jnp.broadcast_to(conf_dw_w[l][:, None, :], (CONF_K, SUBLANES, W_MIX)),
                    conf_dw_b[l].reshape(1, -1), conf_ln_g[l].reshape(1, -1),
                    conf_ln_b[l].reshape(1, -1))
        merged = _gate_lift(h, y, w_in_b, w_branch_b, gate_b, l)
        x2 = _resid_mm(merged, w_out_b, x2, mod3, l, g_idx=2, tm=TM, tn=TN_OUT, tk=d)

        h = _norm_mod(x2, norm_ffn_g[l], mod3, sc_idx=4, sh_idx=3)
        act = _ffn_up(h, up_b, conv_p, l)
        x2 = _resid_mm(act, down_b, x2, mod3, l, g_idx=5, tm=TM, tn=TN_DOWN, tk=TK_DOWN)

    return _final_norm(x2, final_g).reshape(bsz, seq, d)
```

```python
import functools

import jax
import jax.numpy as jnp
from jax import lax
from jax.experimental import pallas as pl
from jax.experimental.pallas import tpu as pltpu

F32 = jnp.float32
BF16 = jnp.bfloat16

D_MODEL = 4096
SEQ = 2048
W_MIX = 1024
N_MIX_IN = 8 * W_MIX
POOL_WINDOWS = (2, 4, 8, 16)
POOL_GROUP = W_MIX // len(POOL_WINDOWS)
CHUNK = 128
SGU_HEADS = W_MIX // CHUNK
SHORT_K = 3
CONF_K = 31
N_BRANCH = 4
FFN_HIDDEN = 11008
FFN_CONV_K = 3
RMS_EPS = 1e-6
LN_EPS = 1e-5

V7X_VMEM_LIMIT = 60000 * 1024
SUBLANES = 8
MXU_COLS = 256

TM = 1024
TN_MIX_PROJ = 1024
TN_GATE = 1024
TN_OUT = 1024
FFN_TN = 512
FFN_PAD = 11264
TN_DOWN = 1024
TK_DOWN = FFN_PAD // 4
T_MIXER = 256
T_NORM = 512
FFN_SRC_BLOCKS = FFN_HIDDEN // MXU_COLS
FFN_PAD_BLOCKS = FFN_PAD // MXU_COLS


def _cparams(sem):
    return pltpu.CompilerParams(dimension_semantics=sem, vmem_limit_bytes=V7X_VMEM_LIMIT)


def _cast_kernel(x_ref, o_ref):
    o_ref[...] = x_ref[...].astype(o_ref.dtype)


def _cast_bf16(w, tr, tc, first=0, count=None):
    n_layers, r, c = w.shape
    count = n_layers - first if count is None else count
    return pl.pallas_call(
        _cast_kernel,
        grid=(count, r // tr, c // tc),
        in_specs=[pl.BlockSpec((None, tr, tc), lambda l, i, j: (first + l, i, j))],
        out_specs=pl.BlockSpec((None, tr, tc), lambda l, i, j: (l, i, j)),
        out_shape=jax.ShapeDtypeStruct((count, r, c), BF16),
        compiler_params=_cparams(("arbitrary",) * 3),
        name="cast_bf16",
    )(w)


def _ffn_up_src(jo):
    per_half = FFN_TN // MXU_COLS
    step, within = jo // (2 * per_half), jo % (2 * per_half)
    half, sub = within // per_half, within % per_half
    c = step * per_half + sub
    return half * FFN_SRC_BLOCKS + jnp.minimum(c, FFN_SRC_BLOCKS - 1), c >= FFN_SRC_BLOCKS


def _cast_ffn_up_kernel(x_ref, o_ref):
    _, is_pad = _ffn_up_src(pl.program_id(1))
    v = x_ref[...].astype(o_ref.dtype)
    o_ref[...] = jnp.where(is_pad, jnp.zeros_like(v), v)


def _cast_ffn_up(w, first=0, count=None):
    n_layers, d, _ = w.shape
    count = n_layers - first if count is None else count
    return pl.pallas_call(
        _cast_ffn_up_kernel,
        grid=(count, 2 * FFN_PAD_BLOCKS),
        in_specs=[pl.BlockSpec((None, d, MXU_COLS), lambda l, jo: (first + l, 0, _ffn_up_src(jo)[0]))],
        out_specs=pl.BlockSpec((None, d, MXU_COLS), lambda l, jo: (l, 0, jo)),
        out_shape=jax.ShapeDtypeStruct((count, d, 2 * FFN_PAD), BF16),
        compiler_params=_cparams(("arbitrary",) * 2),
        name="cast_ffn_up",
    )(w)


def _cast_ffn_down_kernel(x_ref, o_ref):
    v = x_ref[...].astype(o_ref.dtype)
    o_ref[...] = jnp.where(pl.program_id(1) >= FFN_SRC_BLOCKS, jnp.zeros_like(v), v)


def _cast_ffn_down(w):
    n_layers, _, d = w.shape
    return pl.pallas_call(
        _cast_ffn_down_kernel,
        grid=(n_layers, FFN_PAD_BLOCKS),
        in_specs=[pl.BlockSpec((None, MXU_COLS, d),
                               lambda l, i: (l, jnp.minimum(i, FFN_SRC_BLOCKS - 1), 0))],
        out_specs=pl.BlockSpec((None, MXU_COLS, d), lambda l, i: (l, i, 0)),
        out_shape=jax.ShapeDtypeStruct((n_layers, FFN_PAD, d), BF16),
        compiler_params=_cparams(("arbitrary",) * 2),
        name="cast_ffn_down",
    )(w)


def _reblock_ffn_conv(conv):
    n_layers, k, _ = conv.shape
    halves = conv.reshape(n_layers, k, 2, FFN_HIDDEN)
    halves = jnp.pad(halves, ((0, 0), (0, 0), (0, 0), (0, FFN_PAD - FFN_HIDDEN)))
    halves = halves.reshape(n_layers, k, 2, FFN_PAD // FFN_TN, FFN_TN)
    return jnp.swapaxes(halves, 2, 3).reshape(n_layers, k, 2 * FFN_PAD)


def _ada_kernel(c_ref, w_ref, b_ref, o_ref):
    cond = jax.nn.silu(c_ref[...]).astype(BF16)
    o_ref[...] = jnp.dot(cond, w_ref[...].astype(BF16), preferred_element_type=F32) + b_ref[...]


def _ada(c, ada_w, ada_b, tn=1024):
    n_layers, d, n = ada_w.shape
    bsz = c.shape[0]
    return pl.pallas_call(
        _ada_kernel,
        grid=(n_layers, n // tn),
        in_specs=[
            pl.BlockSpec((bsz, d), lambda l, j: (0, 0)),
            pl.BlockSpec((None, d, tn), lambda l, j: (l, 0, j)),
            pl.BlockSpec((None, 1, tn), lambda l, j: (l, 0, j)),
        ],
        out_specs=pl.BlockSpec((None, bsz, tn), lambda l, j: (l, 0, j)),
        out_shape=jax.ShapeDtypeStruct((n_layers, bsz, n), F32),
        compiler_params=_cparams(("arbitrary", "arbitrary")),
        name="ada_mod",
    )(c, ada_w, ada_b.reshape(n_layers, 1, n))


def _norm_mod_kernel(x_ref, g_ref, sc_ref, sh_ref, o_ref):
    x = x_ref[...]
    y = x * lax.rsqrt(jnp.mean(x * x, axis=-1, keepdims=True) + RMS_EPS) * g_ref[...]
    o_ref[...] = (y * (1.0 + sc_ref[...]) + sh_ref[...]).astype(o_ref.dtype)


def _norm_kernel(x_ref, g_ref, o_ref):
    x = x_ref[...]
    y = x * lax.rsqrt(jnp.mean(x * x, axis=-1, keepdims=True) + RMS_EPS) * g_ref[...]
    o_ref[...] = y.astype(o_ref.dtype)


def _norm_mod(x2, g, mod3, sc_idx, sh_idx, tr=T_NORM):
    t, d = x2.shape
    per_seq = SEQ // tr
    return pl.pallas_call(
        _norm_mod_kernel,
        grid=(t // tr,),
        in_specs=[
            pl.BlockSpec((tr, d), lambda i: (i, 0)),
            pl.BlockSpec((1, d), lambda i: (0, 0)),
            pl.BlockSpec((None, 1, d), lambda i: (i // per_seq, 0, sc_idx)),
            pl.BlockSpec((None, 1, d), lambda i: (i // per_seq, 0, sh_idx)),
        ],
        out_specs=pl.BlockSpec((tr, d), lambda i: (i, 0)),
        out_shape=jax.ShapeDtypeStruct((t, d), BF16),
        compiler_params=_cparams(("arbitrary",)),
        name="norm_mod",
    )(x2, g.reshape(1, d), mod3, mod3)


def _final_norm(x2, g, tr=T_NORM):
    t, d = x2.shape
    return pl.pallas_call(
        _norm_kernel,
        grid=(t // tr,),
        in_specs=[
            pl.BlockSpec((tr, d), lambda i: (i, 0)),
            pl.BlockSpec((1, d), lambda i: (0, 0)),
        ],
        out_specs=pl.BlockSpec((tr, d), lambda i: (i, 0)),
        out_shape=jax.ShapeDtypeStruct((t, d), F32),
        compiler_params=_cparams(("arbitrary",)),
        name="final_norm",
    )(x2, g.reshape(1, d))


def _mm_kernel(a_ref, b_ref, o_ref):
    o_ref[...] = jnp.dot(a_ref[...], b_ref[...], preferred_element_type=F32).astype(o_ref.dtype)


def _mix_proj(h, w_in_b, l, tm=TM, tn=TN_MIX_PROJ):
    t, d = h.shape
    return pl.pallas_call(
        _mm_kernel,
        grid=(t // tm, N_MIX_IN // tn),
        in_specs=[
            pl.BlockSpec((tm, d), lambda i, j: (i, 0)),
            pl.BlockSpec((None, d, tn), lambda i, j: (l, 0, j)),
        ],
        out_specs=pl.BlockSpec((tm, tn), lambda i, j: (i, j)),
        out_shape=jax.ShapeDtypeStruct((t, N_MIX_IN), F32),
        compiler_params=_cparams(("arbitrary", "arbitrary")),
        name="mix_proj",
    )(h, w_in_b)


A_HALO = 16
Q_HALO = 8
D_HALO = 32
CONV_ROWS = 16


def _carry_halo(buf, halo, tt, first):
    @pl.when(first)
    def _():
        buf[0:halo, :] = jnp.zeros((halo, buf.shape[1]), buf.dtype)

    @pl.when(jnp.logical_not(first))
    def _():
        buf[0:halo, :] = buf[tt:tt + halo, :]


def _gelu_exact(x):
    return 0.5 * x * (1.0 + lax.erf(x * (2.0 ** -0.5)))


def _layer_norm(v, g, b):
    mu = jnp.mean(v, axis=-1, keepdims=True)
    vc = v - mu
    var = jnp.mean(vc * vc, axis=-1, keepdims=True)
    return vc * lax.rsqrt(var + LN_EPS) * g + b


def _mixer_kernel(p_ref, poolw_ref, pools_ref, slng_ref, slnb_ref, sw_ref, sbt_ref, scw_ref,
                  dww_ref, dwb_ref, clng_ref, clnb_ref, y_ref, abuf, qbuf, dbuf, sbuf, cbuf):
    tt = p_ref.shape[0]
    per_seq = SEQ // tt
    i = pl.program_id(0)
    first = (i % per_seq) == 0
    t0 = (i % per_seq) * tt

    _carry_halo(abuf, A_HALO, tt, first)
    abuf[A_HALO:A_HALO + tt, :] = p_ref[:, 0:W_MIX]
    pos = (lax.broadcasted_iota(jnp.int32, (tt, 1), 0) + t0 + 1).astype(F32)
    for g, w in enumerate(POOL_WINDOWS):
        c0 = g * POOL_GROUP
        cur = abuf[A_HALO:A_HALO + tt, c0:c0 + POOL_GROUP]
        win = cur
        for j in range(1, w):
            win = win + abuf[A_HALO - j:A_HALO - j + tt, c0:c0 + POOL_GROUP]
        pooled = win / jnp.minimum(pos, float(w)) - cur
        ya = jnp.dot(pooled.astype(BF16), poolw_ref[g], preferred_element_type=F32)
        y_ref[:, c0:c0 + POOL_GROUP] = (ya * pools_ref[:, c0:c0 + POOL_GROUP]).astype(y_ref.dtype)

    row = lax.broadcasted_iota(jnp.int32, (CHUNK, CHUNK), 0)
    col = lax.broadcasted_iota(jnp.int32, (CHUNK, CHUNK), 1)
    tril = row >= col
    for c in range(tt // CHUNK):
        r0 = c * CHUNK
        z = _gelu_exact(p_ref[r0:r0 + CHUNK, W_MIX:3 * W_MIX])
        u = z[:, :W_MIX]
        v = _layer_norm(z[:, W_MIX:], slng_ref[...], slnb_ref[...]).astype(BF16)
        for hd in range(SGU_HEADS):
            h0 = hd * CHUNK
            wm = jnp.where(tril, sw_ref[hd], 0.0).astype(BF16)
            sv = jnp.dot(wm, v[:, h0:h0 + CHUNK], preferred_element_type=F32) + sbt_ref[hd]
            y_ref[r0:r0 + CHUNK, W_MIX + h0:W_MIX + h0 + CHUNK] = (
                u[:, h0:h0 + CHUNK] * sv).astype(y_ref.dtype)

    _carry_halo(qbuf, Q_HALO, tt, first)
    q = p_ref[:, 4 * W_MIX:5 * W_MIX] * p_ref[:, 5 * W_MIX:6 * W_MIX]
    qbuf[Q_HALO:Q_HALO + tt, :] = q
    conv = scw_ref[SHORT_K - 1:SHORT_K, :] * q
    for k in range(SHORT_K - 1):
        s = SHORT_K - 1 - k
        conv = conv + scw_ref[k:k + 1, :] * qbuf[Q_HALO - s:Q_HALO - s + tt, :]
    y_ref[:, 2 * W_MIX:3 * W_MIX] = (p_ref[:, 3 * W_MIX:4 * W_MIX] * conv).astype(y_ref.dtype)

    _carry_halo(dbuf, D_HALO, tt, first)
    dbuf[D_HALO:D_HALO + tt, :] = (p_ref[:, 6 * W_MIX:7 * W_MIX]
                                   * jax.nn.sigmoid(p_ref[:, 7 * W_MIX:8 * W_MIX]))
    n_s = tt + D_HALO
    for r in range(SUBLANES):
        sbuf[r, 0:n_s - r, :] = dbuf[r:n_s, :]

    def conv_rows(ci, carry):
        r0 = pl.multiple_of(ci * CONV_ROWS, CONV_ROWS)
        n_blk = CONV_ROWS // SUBLANES
        accs = [jnp.broadcast_to(dwb_ref[...], (SUBLANES, W_MIX))] * n_blk
        for k in range(CONF_K):
            qq, rr = divmod(k + D_HALO - (CONF_K - 1), SUBLANES)
            wk = dww_ref[k]
            accs = [acc + wk * sbuf[rr, pl.ds(r0 + (qq + b) * SUBLANES, SUBLANES), :]
                    for b, acc in enumerate(accs)]
        for b, acc in enumerate(accs):
            cbuf[pl.ds(r0 + b * SUBLANES, SUBLANES), :] = acc
        return carry

    lax.fori_loop(0, tt // CONV_ROWS, conv_rows, 0)
    yd = _layer_norm(cbuf[...], clng_ref[...], clnb_ref[...])
    y_ref[:, 3 * W_MIX:4 * W_MIX] = jax.nn.silu(yd).astype(y_ref.dtype)


def _mixers(p, pool_w_b, pool_scale, sgu_ln_g, sgu_ln_b, sgu_w, sgu_b_t, sconv_w,
            conf_dw_w, conf_dw_b, conf_ln_g, conf_ln_b, tt=T_MIXER):
    t = p.shape[0]

    def full(a):
        nd = a.ndim
        return pl.BlockSpec(a.shape, lambda i: (0,) * nd)

    small = (pool_w_b, pool_scale, sgu_ln_g, sgu_ln_b, sgu_w, sgu_b_t, sconv_w,
             conf_dw_w, conf_dw_b, conf_ln_g, conf_ln_b)
    return pl.pallas_call(
        _mixer_kernel,
        grid=(t // tt,),
        in_specs=[pl.BlockSpec((tt, N_MIX_IN), lambda i: (i, 0))] + [full(a) for a in small],
        out_specs=pl.BlockSpec((tt, N_BRANCH * W_MIX), lambda i: (i, 0)),
        out_shape=jax.ShapeDtypeStruct((t, N_BRANCH * W_MIX), BF16),
        scratch_shapes=[
            pltpu.VMEM((A_HALO + tt, W_MIX), F32),
            pltpu.VMEM((Q_HALO + tt, W_MIX), F32),
            pltpu.VMEM((D_HALO + tt, W_MIX), F32),
            pltpu.VMEM((SUBLANES, D_HALO + tt, W_MIX), F32),
            pltpu.VMEM((tt, W_MIX), F32),
        ],
        compiler_params=_cparams(("arbitrary",)),
        name="mixers",
    )(p, *small)


def _gate_lift_kernel(h_ref, y_ref, wg_ref, wb_ref, gb_ref, o_ref, acc_ref):
    br = pl.program_id(2)

    def term():
        gate = jnp.dot(h_ref[...], wg_ref[...], preferred_element_type=F32) + gb_ref[...]
        lift = jnp.dot(y_ref[...], wb_ref[...], preferred_element_type=F32)
        return jax.nn.sigmoid(gate) * lift

    @pl.when(br == 0)
    def _():
        acc_ref[...] = term()

    @pl.when(jnp.logical_and(br > 0, br < N_BRANCH - 1))
    def _():
        acc_ref[...] += term()

    @pl.when(br == N_BRANCH - 1)
    def _():
        o_ref[...] = (acc_ref[...] + term()).astype(o_ref.dtype)


def _gate_lift(h, y, w_in_b, l_in, w_branch_b, gate_b, l, tm=TM, tn=TN_GATE):
    t, d = h.shape
    nj = d // tn
    gate_col0 = N_MIX_IN // tn
    return pl.pallas_call(
        _gate_lift_kernel,
        grid=(t // tm, nj, N_BRANCH),
        in_specs=[
            pl.BlockSpec((tm, d), lambda i, j, b: (i, 0)),
            pl.BlockSpec((tm, W_MIX), lambda i, j, b: (i, b)),
            pl.BlockSpec((None, d, tn), lambda i, j, b: (l_in, 0, gate_col0 + b * nj + j)),
            pl.BlockSpec((None, W_MIX, tn), lambda i, j, b: (l, b, j)),
            pl.BlockSpec((None, 1, tn), lambda i, j, b: (l, 0, b * nj + j)),
        ],
        out_specs=pl.BlockSpec((tm, tn), lambda i, j, b: (i, j)),
        out_shape=jax.ShapeDtypeStruct((t, d), BF16),
        scratch_shapes=[pltpu.VMEM((tm, tn), F32)],
        compiler_params=_cparams(("arbitrary", "arbitrary", "arbitrary")),
        name="gate_lift",
    )(h, y, w_in_b, w_branch_b, gate_b.reshape(gate_b.shape[0], 1, -1))


def _resid_mm_kernel(a_ref, w_ref, x_ref, g_ref, *rest, nk, with_cast):
    if with_cast:
        src_ref, o_ref, dst_ref, acc_ref = rest
    else:
        o_ref, acc_ref = rest

    def part():
        if with_cast:
            dst_ref[...] = src_ref[...].astype(dst_ref.dtype)
        return jnp.dot(a_ref[...], w_ref[...], preferred_element_type=F32)

    if nk == 1:
        o_ref[...] = x_ref[...] + g_ref[...] * part()
        return

    k = pl.program_id(2)

    @pl.when(k == 0)
    def _():
        acc_ref[...] = part()

    @pl.when(jnp.logical_and(k > 0, k < nk - 1))
    def _():
        acc_ref[...] += part()

    @pl.when(k == nk - 1)
    def _():
        o_ref[...] = x_ref[...] + g_ref[...] * (acc_ref[...] + part())


def _resid_mm(a, w_b, x2, mod3, l, g_idx, tm, tn, tk, cast_src=None, cast_layer=None):
    t, kdim = a.shape
    d = w_b.shape[2]
    nk = kdim // tk
    per_seq = SEQ // tm
    nj = d // tn
    grid = (t // tm, nj, nk)
    in_specs = [
        pl.BlockSpec((tm, tk), lambda i, j, k: (i, k)),
        pl.BlockSpec((None, tk, tn), lambda i, j, k: (l, k, j)),
        pl.BlockSpec((tm, tn), lambda i, j, k: (i, j)),
        pl.BlockSpec((None, 1, tn), lambda i, j, k: (i // per_seq, 0, g_idx * nj + j)),
    ]
    out_specs = pl.BlockSpec((tm, tn), lambda i, j, k: (i, j))
    out_shape = jax.ShapeDtypeStruct((t, d), F32)
    operands = [a, w_b, x2, mod3]
    if cast_src is not None:
        n_steps = grid[0] * grid[1] * grid[2]
        _, rows, cols = cast_src.shape
        slab = rows // n_steps
        assert slab * n_steps == rows and slab % (2 * SUBLANES) == 0

        def step(i, j, k):
            return (i * nj + j) * nk + k

        in_specs.append(pl.BlockSpec((None, slab, cols), lambda i, j, k: (cast_layer, step(i, j, k), 0)))
        out_specs = [out_specs, pl.BlockSpec((None, slab, cols), lambda i, j, k: (0, step(i, j, k), 0))]
        out_shape = [out_shape, jax.ShapeDtypeStruct((1, rows, cols), BF16)]
        operands.append(cast_src)
    return pl.pallas_call(
        functools.partial(_resid_mm_kernel, nk=nk, with_cast=cast_src is not None),
        grid=grid,
        in_specs=in_specs,
        out_specs=out_specs,
        out_shape=out_shape,
        scratch_shapes=[pltpu.VMEM((tm, tn) if nk > 1 else (SUBLANES, 128), F32)],
        compiler_params=_cparams(("arbitrary", "arbitrary", "arbitrary")),
        name="resid_mm",
    )(*operands)


Z_HALO = 8


def _ffn_up_kernel(h_ref, w_ref, c_ref, *rest, n_time_tiles, cast_row_chunks):
    if cast_row_chunks:
        src_ref, o_ref, dst_ref, zbuf = rest
        step = pl.program_id(0) * n_time_tiles + pl.program_id(1)
        _, is_pad = _ffn_up_src(step // cast_row_chunks)
        v = src_ref[...].astype(dst_ref.dtype)
        dst_ref[...] = jnp.where(is_pad, jnp.zeros_like(v), v)
    else:
        o_ref, zbuf = rest
    tm = h_ref.shape[0]
    tn = o_ref.shape[1]
    per_seq = SEQ // tm

    @pl.when((pl.program_id(1) % per_seq) == 0)
    def _():
        zbuf[tm:tm + Z_HALO, :] = jnp.zeros((Z_HALO, zbuf.shape[1]), F32)

    z = jnp.dot(h_ref[...], w_ref[...], preferred_element_type=F32)
    zbuf[0:Z_HALO, :] = zbuf[tm:tm + Z_HALO, :]
    zbuf[Z_HALO:Z_HALO + tm, :] = z
    conv = c_ref[FFN_CONV_K - 1:FFN_CONV_K, :] * z
    for k in range(FFN_CONV_K - 1):
        s = FFN_CONV_K - 1 - k
        conv = conv + c_ref[k:k + 1, :] * zbuf[Z_HALO - s:Z_HALO - s + tm, :]
    o_ref[...] = (jax.nn.silu(conv[:, :tn]) * conv[:, tn:]).astype(o_ref.dtype)


def _ffn_up(h, up_b, l_up, conv_p, l, tm=TM, tn=FFN_TN, cast_src=None, cast_layer=None):
    t, d = h.shape
    grid = (FFN_PAD // tn, t // tm)
    in_specs = [
        pl.BlockSpec((tm, d), lambda j, i: (i, 0)),
        pl.BlockSpec((None, d, 2 * tn), lambda j, i: (l_up, 0, j)),
        pl.BlockSpec((None, FFN_CONV_K, 2 * tn), lambda j, i: (l, 0, j)),
    ]
    out_specs = pl.BlockSpec((tm, tn), lambda j, i: (i, j))
    out_shape = jax.ShapeDtypeStruct((t, FFN_PAD), BF16)
    operands = [h, up_b, conv_p]
    chunks = 0
    if cast_src is not None:
        n_blocks = 2 * FFN_PAD_BLOCKS
        chunks = (grid[0] * grid[1]) // n_blocks
        rows = cast_src.shape[1]
        assert chunks * n_blocks == grid[0] * grid[1] and rows % (chunks * 2 * SUBLANES) == 0
        slab = rows // chunks

        def step(j, i):
            return j * grid[1] + i

        in_specs.append(pl.BlockSpec(
            (None, slab, MXU_COLS),
            lambda j, i: (cast_layer, step(j, i) % chunks, _ffn_up_src(step(j, i) // chunks)[0])))
        out_specs = [out_specs, pl.BlockSpec(
            (None, slab, MXU_COLS), lambda j, i: (0, step(j, i) % chunks, step(j, i) // chunks))]
        out_shape = [out_shape, jax.ShapeDtypeStruct((1, rows, 2 * FFN_PAD), BF16)]
        operands.append(cast_src)
    return pl.pallas_call(
        functools.partial(_ffn_up_kernel, n_time_tiles=grid[1], cast_row_chunks=chunks),
        grid=grid,
        in_specs=in_specs,
        out_specs=out_specs,
        out_shape=out_shape,
        scratch_shapes=[pltpu.VMEM((Z_HALO + tm, 2 * tn), F32)],
        compiler_params=_cparams(("arbitrary", "arbitrary")),
        name="ffn_up",
    )(*operands)


def kernel(x, c, ada_w, ada_b, norm_mix_g, w_in, pool_w, pool_scale, sgu_ln_g, sgu_ln_b, sgu_w,
           sgu_b, sconv_w, conf_dw_w, conf_dw_b, conf_ln_g, conf_ln_b, gate_b, w_branch, w_out,
           norm_ffn_g, ffn_up, ffn_conv, ffn_down, final_g):
    bsz, seq, d = x.shape
    n_layers = ada_w.shape[0]
    assert (seq, d) == (SEQ, D_MODEL)
    x2 = x.reshape(bsz * seq, d)

    mod = _ada(c, ada_w, ada_b)
    w_branch_b = _cast_bf16(w_branch, 512, 4096)
    w_out_b = _cast_bf16(w_out, 512, 4096)
    down_b = _cast_ffn_down(ffn_down)
    conv_p = _reblock_ffn_conv(ffn_conv)
    w_in_b = _cast_bf16(w_in, 512, 4096, first=0, count=1)
    up_b = _cast_ffn_up(ffn_up, first=0, count=1)

    for l in range(n_layers):
        nxt = l + 1 if l + 1 < n_layers else None
        mod3 = mod[l].reshape(bsz, 1, 6 * d)

        h = _norm_mod(x2, norm_mix_g[l], mod3, sc_idx=1, sh_idx=0)
        p = _mix_proj(h, w_in_b, 0)
        y = _mixers(p, pool_w[l].astype(BF16), pool_scale[l].reshape(1, -1),
                    sgu_ln_g[l].reshape(1, -1), sgu_ln_b[l].reshape(1, -1), sgu_w[l],
                    sgu_b[l][:, :, None], sconv_w[l],
                    jnp.broadcast_to(conf_dw_w[l][:, None, :], (CONF_K, SUBLANES, W_MIX)),
                    conf_dw_b[l].reshape(1, -1), conf_ln_g[l].reshape(1, -1),
                    conf_ln_b[l].reshape(1, -1))
        merged = _gate_lift(h, y, w_in_b, 0, w_branch_b, gate_b, l)
        x2 = _resid_mm(merged, w_out_b, x2, mod3, l, g_idx=2, tm=TM, tn=TN_OUT, tk=d)

        h = _norm_mod(x2, norm_ffn_g[l], mod3, sc_idx=4, sh_idx=3)
        if nxt is None:
            act = _ffn_up(h, up_b, 0, conv_p, l)
            x2 = _resid_mm(act, down_b, x2, mod3, l, g_idx=5, tm=TM, tn=TN_DOWN, tk=TK_DOWN)
        else:
            act, up_b = _ffn_up(h, up_b, 0, conv_p, l, cast_src=ffn_up, cast_layer=nxt)
            x2, w_in_b = _resid_mm(act, down_b, x2, mod3, l, g_idx=5, tm=TM, tn=TN_DOWN,
                                   tk=TK_DOWN, cast_src=w_in, cast_layer=nxt)

    return _final_norm(x2, final_g).reshape(bsz, seq, d)
```

```python
import functools

import jax
import jax.numpy as jnp
from jax import lax
from jax.experimental import pallas as pl
from jax.experimental.pallas import tpu as pltpu

F32 = jnp.float32
BF16 = jnp.bfloat16

D_MODEL = 4096
SEQ = 2048
W_MIX = 1024
N_MIX_IN = 8 * W_MIX
POOL_WINDOWS = (2, 4, 8, 16)
POOL_GROUP = W_MIX // len(POOL_WINDOWS)
CHUNK = 128
SGU_HEADS = W_MIX // CHUNK
SHORT_K = 3
CONF_K = 31
N_BRANCH = 4
FFN_HIDDEN = 11008
FFN_CONV_K = 3
RMS_EPS = 1e-6
LN_EPS = 1e-5

V7X_VMEM_LIMIT = 60000 * 1024
SUBLANES = 8
MXU_COLS = 256

TM = 1024
TN_MIX_PROJ = 1024
TN_GATE = 1024
TN_OUT = 1024
FFN_TN = 512
FFN_PAD = 11264
TN_DOWN = 1024
TK_DOWN = FFN_PAD // 4
T_MIXER = 256
T_NORM = 512
FFN_SRC_BLOCKS = FFN_HIDDEN // MXU_COLS
FFN_PAD_BLOCKS = FFN_PAD // MXU_COLS


def _cparams(sem):
    return pltpu.CompilerParams(dimension_semantics=sem, vmem_limit_bytes=V7X_VMEM_LIMIT)


def _cast_kernel(x_ref, o_ref):
    o_ref[...] = x_ref[...].astype(o_ref.dtype)


def _cast_bf16(w, tr, tc, first=0, count=None):
    n_layers, r, c = w.shape
    count = n_layers - first if count is None else count
    return pl.pallas_call(
        _cast_kernel,
        grid=(count, r // tr, c // tc),
        in_specs=[pl.BlockSpec((None, tr, tc), lambda l, i, j: (first + l, i, j))],
        out_specs=pl.BlockSpec((None, tr, tc), lambda l, i, j: (l, i, j)),
        out_shape=jax.ShapeDtypeStruct((count, r, c), BF16),
        compiler_params=_cparams(("arbitrary",) * 3),
        name="cast_bf16",
    )(w)


def _ffn_up_src(jo):
    per_half = FFN_TN // MXU_COLS
    step, within = jo // (2 * per_half), jo % (2 * per_half)
    half, sub = within // per_half, within % per_half
    c = step * per_half + sub
    return half * FFN_SRC_BLOCKS + jnp.minimum(c, FFN_SRC_BLOCKS - 1), c >= FFN_SRC_BLOCKS


def _cast_ffn_up_kernel(x_ref, o_ref):
    _, is_pad = _ffn_up_src(pl.program_id(1))
    v = x_ref[...].astype(o_ref.dtype)
    o_ref[...] = jnp.where(is_pad, jnp.zeros_like(v), v)


def _cast_ffn_up(w, first=0, count=None):
    n_layers, d, _ = w.shape
    count = n_layers - first if count is None else count
    return pl.pallas_call(
        _cast_ffn_up_kernel,
        grid=(count, 2 * FFN_PAD_BLOCKS),
        in_specs=[pl.BlockSpec((None, d, MXU_COLS), lambda l, jo: (first + l, 0, _ffn_up_src(jo)[0]))],
        out_specs=pl.BlockSpec((None, d, MXU_COLS), lambda l, jo: (l, 0, jo)),
        out_shape=jax.ShapeDtypeStruct((count, d, 2 * FFN_PAD), BF16),
        compiler_params=_cparams(("arbitrary",) * 2),
        name="cast_ffn_up",
    )(w)


def _cast_ffn_down_kernel(x_ref, o_ref):
    v = x_ref[...].astype(o_ref.dtype)
    o_ref[...] = jnp.where(pl.program_id(1) >= FFN_SRC_BLOCKS, jnp.zeros_like(v), v)


def _cast_ffn_down(w, first=0, count=None):
    n_layers, _, d = w.shape
    count = n_layers - first if count is None else count
    return pl.pallas_call(
        _cast_ffn_down_kernel,
        grid=(count, FFN_PAD_BLOCKS),
        in_specs=[pl.BlockSpec((None, MXU_COLS, d),
                               lambda l, i: (first + l, jnp.minimum(i, FFN_SRC_BLOCKS - 1), 0))],
        out_specs=pl.BlockSpec((None, MXU_COLS, d), lambda l, i: (l, i, 0)),
        out_shape=jax.ShapeDtypeStruct((count, FFN_PAD, d), BF16),
        compiler_params=_cparams(("arbitrary",) * 2),
        name="cast_ffn_down",
    )(w)


def _reblock_ffn_conv(conv):
    n_layers, k, _ = conv.shape
    halves = conv.reshape(n_layers, k, 2, FFN_HIDDEN)
    halves = jnp.pad(halves, ((0, 0), (0, 0), (0, 0), (0, FFN_PAD - FFN_HIDDEN)))
    halves = halves.reshape(n_layers, k, 2, FFN_PAD // FFN_TN, FFN_TN)
    return jnp.swapaxes(halves, 2, 3).reshape(n_layers, k, 2 * FFN_PAD)


def _ada_kernel(c_ref, w_ref, b_ref, o_ref):
    cond = jax.nn.silu(c_ref[...]).astype(BF16)
    o_ref[...] = jnp.dot(cond, w_ref[...].astype(BF16), preferred_element_type=F32) + b_ref[...]


def _ada(c, ada_w, ada_b, tn=1024):
    n_layers, d, n = ada_w.shape
    bsz = c.shape[0]
    return pl.pallas_call(
        _ada_kernel,
        grid=(n_layers, n // tn),
        in_specs=[
            pl.BlockSpec((bsz, d), lambda l, j: (0, 0)),
            pl.BlockSpec((None, d, tn), lambda l, j: (l, 0, j)),
            pl.BlockSpec((None, 1, tn), lambda l, j: (l, 0, j)),
        ],
        out_specs=pl.BlockSpec((None, bsz, tn), lambda l, j: (l, 0, j)),
        out_shape=jax.ShapeDtypeStruct((n_layers, bsz, n), F32),
        compiler_params=_cparams(("arbitrary", "arbitrary")),
        name="ada_mod",
    )(c, ada_w, ada_b.reshape(n_layers, 1, n))


def _norm_mod_kernel(x_ref, g_ref, sc_ref, sh_ref, o_ref):
    x = x_ref[...]
    y = x * lax.rsqrt(jnp.mean(x * x, axis=-1, keepdims=True) + RMS_EPS) * g_ref[...]
    o_ref[...] = (y * (1.0 + sc_ref[...]) + sh_ref[...]).astype(o_ref.dtype)


def _norm_kernel(x_ref, g_ref, o_ref):
    x = x_ref[...]
    y = x * lax.rsqrt(jnp.mean(x * x, axis=-1, keepdims=True) + RMS_EPS) * g_ref[...]
    o_ref[...] = y.astype(o_ref.dtype)


def _norm_mod(x2, g, mod3, sc_idx, sh_idx, tr=T_NORM):
    t, d = x2.shape
    per_seq = SEQ // tr
    return pl.pallas_call(
        _norm_mod_kernel,
        grid=(t // tr,),
        in_specs=[
            pl.BlockSpec((tr, d), lambda i: (i, 0)),
            pl.BlockSpec((1, d), lambda i: (0, 0)),
            pl.BlockSpec((None, 1, d), lambda i: (i // per_seq, 0, sc_idx)),
            pl.BlockSpec((None, 1, d), lambda i: (i // per_seq, 0, sh_idx)),
        ],
        out_specs=pl.BlockSpec((tr, d), lambda i: (i, 0)),
        out_shape=jax.ShapeDtypeStruct((t, d), BF16),
        compiler_params=_cparams(("arbitrary",)),
        name="norm_mod",
    )(x2, g.reshape(1, d), mod3, mod3)


def _final_norm(x2, g, tr=T_NORM):
    t, d = x2.shape
    return pl.pallas_call(
        _norm_kernel,
        grid=(t // tr,),
        in_specs=[
            pl.BlockSpec((tr, d), lambda i: (i, 0)),
            pl.BlockSpec((1, d), lambda i: (0, 0)),
        ],
        out_specs=pl.BlockSpec((tr, d), lambda i: (i, 0)),
        out_shape=jax.ShapeDtypeStruct((t, d), F32),
        compiler_params=_cparams(("arbitrary",)),
        name="final_norm",
    )(x2, g.reshape(1, d))


def _mm_kernel(a_ref, b_ref, *rest):
    n_cast = (len(rest) - 1) // 2
    o_ref = rest[n_cast]
    for src_ref, dst_ref in zip(rest[:n_cast], rest[n_cast + 1:]):
        dst_ref[...] = src_ref[...].astype(dst_ref.dtype)
    o_ref[...] = jnp.dot(a_ref[...], b_ref[...], preferred_element_type=F32).astype(o_ref.dtype)


def _mix_proj(h, w_in_b, l, tm=TM, tn=TN_MIX_PROJ, cast_srcs=(), cast_layer=None):
    t, d = h.shape
    nj = N_MIX_IN // tn
    grid = (t // tm, nj)
    in_specs = [
        pl.BlockSpec((tm, d), lambda i, j: (i, 0)),
        pl.BlockSpec((None, d, tn), lambda i, j: (l, 0, j)),
    ]
    out_specs = [pl.BlockSpec((tm, tn), lambda i, j: (i, j))]
    out_shape = [jax.ShapeDtypeStruct((t, N_MIX_IN), F32)]
    for src in cast_srcs:
        _, rows, cols = src.shape
        slab = rows // (grid[0] * grid[1])
        assert slab * grid[0] * grid[1] == rows and slab % (2 * SUBLANES) == 0
        in_specs.append(pl.BlockSpec((None, slab, cols), lambda i, j: (cast_layer, i * nj + j, 0)))
        out_specs.append(pl.BlockSpec((None, slab, cols), lambda i, j: (0, i * nj + j, 0)))
        out_shape.append(jax.ShapeDtypeStruct((1, rows, cols), BF16))
    outs = pl.pallas_call(
        _mm_kernel,
        grid=grid,
        in_specs=in_specs,
        out_specs=out_specs,
        out_shape=out_shape,
        compiler_params=_cparams(("arbitrary", "arbitrary")),
        name="mix_proj",
    )(h, w_in_b, *cast_srcs)
    return outs[0] if not cast_srcs else outs


A_HALO = 16
Q_HALO = 8
D_HALO = 32
CONV_ROWS = 16


def _carry_halo(buf, halo, tt, first):
    @pl.when(first)
    def _():
        buf[0:halo, :] = jnp.zeros((halo, buf.shape[1]), buf.dtype)

    @pl.when(jnp.logical_not(first))
    def _():
        buf[0:halo, :] = buf[tt:tt + halo, :]


def _gelu_exact(x):
    return 0.5 * x * (1.0 + lax.erf(x * (2.0 ** -0.5)))


def _layer_norm(v, g, b):
    mu = jnp.mean(v, axis=-1, keepdims=True)
    vc = v - mu
    var = jnp.mean(vc * vc, axis=-1, keepdims=True)
    return vc * lax.rsqrt(var + LN_EPS) * g + b


def _mixer_kernel(p_ref, poolw_ref, pools_ref, slng_ref, slnb_ref, sw_ref, sbt_ref, scw_ref,
                  dww_ref, dwb_ref, clng_ref, clnb_ref, *rest, with_cast):
    if with_cast:
        src_ref, y_ref, dst_ref, abuf, qbuf, dbuf, sbuf, cbuf = rest
        v = src_ref[...].astype(dst_ref.dtype)
        dst_ref[...] = jnp.where(pl.program_id(0) >= FFN_SRC_BLOCKS, jnp.zeros_like(v), v)
    else:
        y_ref, abuf, qbuf, dbuf, sbuf, cbuf = rest
    tt = p_ref.shape[0]
    per_seq = SEQ // tt
    i = pl.program_id(0)
    first = (i % per_seq) == 0
    t0 = (i % per_seq) * tt

    _carry_halo(abuf, A_HALO, tt, first)
    abuf[A_HALO:A_HALO + tt, :] = p_ref[:, 0:W_MIX]
    pos = (lax.broadcasted_iota(jnp.int32, (tt, 1), 0) + t0 + 1).astype(F32)
    for g, w in enumerate(POOL_WINDOWS):
        c0 = g * POOL_GROUP
        cur = abuf[A_HALO:A_HALO + tt, c0:c0 + POOL_GROUP]
        win = cur
        for j in range(1, w):
            win = win + abuf[A_HALO - j:A_HALO - j + tt, c0:c0 + POOL_GROUP]
        pooled = win / jnp.minimum(pos, float(w)) - cur
        ya = jnp.dot(pooled.astype(BF16), poolw_ref[g], preferred_element_type=F32)
        y_ref[:, c0:c0 + POOL_GROUP] = (ya * pools_ref[:, c0:c0 + POOL_GROUP]).astype(y_ref.dtype)

    row = lax.broadcasted_iota(jnp.int32, (CHUNK, CHUNK), 0)
    col = lax.broadcasted_iota(jnp.int32, (CHUNK, CHUNK), 1)
    tril = row >= col
    for c in range(tt // CHUNK):
        r0 = c * CHUNK
        z = _gelu_exact(p_ref[r0:r0 + CHUNK, W_MIX:3 * W_MIX])
        u = z[:, :W_MIX]
        v = _layer_norm(z[:, W_MIX:], slng_ref[...], slnb_ref[...]).astype(BF16)
        for hd in range(SGU_HEADS):
            h0 = hd * CHUNK
            wm = jnp.where(tril, sw_ref[hd], 0.0).astype(BF16)
            sv = jnp.dot(wm, v[:, h0:h0 + CHUNK], preferred_element_type=F32) + sbt_ref[hd]
            y_ref[r0:r0 + CHUNK, W_MIX + h0:W_MIX + h0 + CHUNK] = (
                u[:, h0:h0 + CHUNK] * sv).astype(y_ref.dtype)

    _carry_halo(qbuf, Q_HALO, tt, first)
    q = p_ref[:, 4 * W_MIX:5 * W_MIX] * p_ref[:, 5 * W_MIX:6 * W_MIX]
    qbuf[Q_HALO:Q_HALO + tt, :] = q
    conv = scw_ref[SHORT_K - 1:SHORT_K, :] * q
    for k in range(SHORT_K - 1):
        s = SHORT_K - 1 - k
        conv = conv + scw_ref[k:k + 1, :] * qbuf[Q_HALO - s:Q_HALO - s + tt, :]
    y_ref[:, 2 * W_MIX:3 * W_MIX] = (p_ref[:, 3 * W_MIX:4 * W_MIX] * conv).astype(y_ref.dtype)

    _carry_halo(dbuf, D_HALO, tt, first)
    dbuf[D_HALO:D_HALO + tt, :] = (p_ref[:, 6 * W_MIX:7 * W_MIX]
                                   * jax.nn.sigmoid(p_ref[:, 7 * W_MIX:8 * W_MIX]))
    n_s = tt + D_HALO
    for r in range(SUBLANES):
        sbuf[r, 0:n_s - r, :] = dbuf[r:n_s, :]

    def conv_rows(ci, carry):
        r0 = pl.multiple_of(ci * CONV_ROWS, CONV_ROWS)
        n_blk = CONV_ROWS // SUBLANES
        accs = [jnp.broadcast_to(dwb_ref[...], (SUBLANES, W_MIX))] * n_blk
        for k in range(CONF_K):
            qq, rr = divmod(k + D_HALO - (CONF_K - 1), SUBLANES)
            wk = dww_ref[k]
            accs = [acc + wk * sbuf[rr, pl.ds(r0 + (qq + b) * SUBLANES, SUBLANES), :]
                    for b, acc in enumerate(accs)]
        for b, acc in enumerate(accs):
            cbuf[pl.ds(r0 + b * SUBLANES, SUBLANES), :] = acc
        return carry

    lax.fori_loop(0, tt // CONV_ROWS, conv_rows, 0)
    yd = _layer_norm(cbuf[...], clng_ref[...], clnb_ref[...])
    y_ref[:, 3 * W_MIX:4 * W_MIX] = jax.nn.silu(yd).astype(y_ref.dtype)


def _mixers(p, pool_w_b, pool_scale, sgu_ln_g, sgu_ln_b, sgu_w, sgu_b_t, sconv_w,
            conf_dw_w, conf_dw_b, conf_ln_g, conf_ln_b, tt=T_MIXER, cast_src=None, cast_layer=None):
    t = p.shape[0]

    def full(a):
        nd = a.ndim
        return pl.BlockSpec(a.shape, lambda i: (0,) * nd)

    small = (pool_w_b, pool_scale, sgu_ln_g, sgu_ln_b, sgu_w, sgu_b_t, sconv_w,
             conf_dw_w, conf_dw_b, conf_ln_g, conf_ln_b)
    in_specs = [pl.BlockSpec((tt, N_MIX_IN), lambda i: (i, 0))] + [full(a) for a in small]
    out_specs = pl.BlockSpec((tt, N_BRANCH * W_MIX), lambda i: (i, 0))
    out_shape = jax.ShapeDtypeStruct((t, N_BRANCH * W_MIX), BF16)
    operands = [p, *small]
    if cast_src is not None:
        assert t // tt >= FFN_PAD_BLOCKS
        cols = cast_src.shape[2]
        in_specs.append(pl.BlockSpec(
            (None, MXU_COLS, cols), lambda i: (cast_layer, jnp.minimum(i, FFN_SRC_BLOCKS - 1), 0)))
        out_specs = [out_specs, pl.BlockSpec(
            (None, MXU_COLS, cols), lambda i: (0, jnp.minimum(i, FFN_PAD_BLOCKS - 1), 0))]
        out_shape = [out_shape, jax.ShapeDtypeStruct((1, FFN_PAD, cols), BF16)]
        operands.append(cast_src)
    return pl.pallas_call(
        functools.partial(_mixer_kernel, with_cast=cast_src is not None),
        grid=(t // tt,),
        in_specs=in_specs,
        out_specs=out_specs,
        out_shape=out_shape,
        scratch_shapes=[
            pltpu.VMEM((A_HALO + tt, W_MIX), F32),
            pltpu.VMEM((Q_HALO + tt, W_MIX), F32),
            pltpu.VMEM((D_HALO + tt, W_MIX), F32),
            pltpu.VMEM((SUBLANES, D_HALO + tt, W_MIX), F32),
            pltpu.VMEM((tt, W_MIX), F32),
        ],
        compiler_params=_cparams(("arbitrary",)),
        name="mixers",
    )(*operands)


def _gate_lift_kernel(h_ref, y_ref, wg_ref, wb_ref, gb_ref, o_ref, acc_ref):
    br = pl.program_id(2)

    def term():
        gate = jnp.dot(h_ref[...], wg_ref[...], preferred_element_type=F32) + gb_ref[...]
        lift = jnp.dot(y_ref[...], wb_ref[...], preferred_element_type=F32)
        return jax.nn.sigmoid(gate) * lift

    @pl.when(br == 0)
    def _():
        acc_ref[...] = term()

    @pl.when(jnp.logical_and(br > 0, br < N_BRANCH - 1))
    def _():
        acc_ref[...] += term()

    @pl.when(br == N_BRANCH - 1)
    def _():
        o_ref[...] = (acc_ref[...] + term()).astype(o_ref.dtype)


def _gate_lift(h, y, w_in_b, l_in, w_branch_b, l_br, gate_b, l, tm=TM, tn=TN_GATE):
    t, d = h.shape
    nj = d // tn
    gate_col0 = N_MIX_IN // tn
    return pl.pallas_call(
        _gate_lift_kernel,
        grid=(t // tm, nj, N_BRANCH),
        in_specs=[
            pl.BlockSpec((tm, d), lambda i, j, b: (i, 0)),
            pl.BlockSpec((tm, W_MIX), lambda i, j, b: (i, b)),
            pl.BlockSpec((None, d, tn), lambda i, j, b: (l_in, 0, gate_col0 + b * nj + j)),
            pl.BlockSpec((None, W_MIX, tn), lambda i, j, b: (l_br, b, j)),
            pl.BlockSpec((None, 1, tn), lambda i, j, b: (l, 0, b * nj + j)),
        ],
        out_specs=pl.BlockSpec((tm, tn), lambda i, j, b: (i, j)),
        out_shape=jax.ShapeDtypeStruct((t, d), BF16),
        scratch_shapes=[pltpu.VMEM((tm, tn), F32)],
        compiler_params=_cparams(("arbitrary", "arbitrary", "arbitrary")),
        name="gate_lift",
    )(h, y, w_in_b, w_branch_b, gate_b.reshape(gate_b.shape[0], 1, -1))


def _resid_mm_kernel(a_ref, w_ref, x_ref, g_ref, *rest, nk, with_cast):
    if with_cast:
        src_ref, o_ref, dst_ref, acc_ref = rest
    else:
        o_ref, acc_ref = rest

    def part():
        if with_cast:
            dst_ref[...] = src_ref[...].astype(dst_ref.dtype)
        return jnp.dot(a_ref[...], w_ref[...], preferred_element_type=F32)

    if nk == 1:
        o_ref[...] = x_ref[...] + g_ref[...] * part()
        return

    k = pl.program_id(2)

    @pl.when(k == 0)
    def _():
        acc_ref[...] = part()

    @pl.when(jnp.logical_and(k > 0, k < nk - 1))
    def _():
        acc_ref[...] += part()

    @pl.when(k == nk - 1)
    def _():
        o_ref[...] = x_ref[...] + g_ref[...] * (acc_ref[...] + part())


def _resid_mm(a, w_b, x2, mod3, l, g_idx, tm, tn, tk, cast_src=None, cast_layer=None):
    t, kdim = a.shape
    d = w_b.shape[2]
    nk = kdim // tk
    per_seq = SEQ // tm
    nj = d // tn
    grid = (t // tm, nj, nk)
    in_specs = [
        pl.BlockSpec((tm, tk), lambda i, j, k: (i, k)),
        pl.BlockSpec((None, tk, tn), lambda i, j, k: (l, k, j)),
        pl.BlockSpec((tm, tn), lambda i, j, k: (i, j)),
        pl.BlockSpec((None, 1, tn), lambda i, j, k: (i // per_seq, 0, g_idx * nj + j)),
    ]
    out_specs = pl.BlockSpec((tm, tn), lambda i, j, k: (i, j))
    out_shape = jax.ShapeDtypeStruct((t, d), F32)
    operands = [a, w_b, x2, mod3]
    if cast_src is not None:
        n_steps = grid[0] * grid[1] * grid[2]
        _, rows, cols = cast_src.shape
        slab = rows // n_steps
        assert slab * n_steps == rows and slab % (2 * SUBLANES) == 0

        def step(i, j, k):
            return (i * nj + j) * nk + k

        in_specs.append(pl.BlockSpec((None, slab, cols), lambda i, j, k: (cast_layer, step(i, j, k), 0)))
        out_specs = [out_specs, pl.BlockSpec((None, slab, cols), lambda i, j, k: (0, step(i, j, k), 0))]
        out_shape = [out_shape, jax.ShapeDtypeStruct((1, rows, cols), BF16)]
        operands.append(cast_src)
    return pl.pallas_call(
        functools.partial(_resid_mm_kernel, nk=nk, with_cast=cast_src is not None),
        grid=grid,
        in_specs=in_specs,
        out_specs=out_specs,
        out_shape=out_shape,
        scratch_shapes=[pltpu.VMEM((tm, tn) if nk > 1 else (SUBLANES, 128), F32)],
        compiler_params=_cparams(("arbitrary", "arbitrary", "arbitrary")),
        name="resid_mm",
    )(*operands)


Z_HALO = 8


def _ffn_up_kernel(h_ref, w_ref, c_ref, *rest, n_time_tiles, cast_row_chunks):
    if cast_row_chunks:
        src_ref, o_ref, dst_ref, zbuf = rest
        step = pl.program_id(0) * n_time_tiles + pl.program_id(1)
        _, is_pad = _ffn_up_src(step // cast_row_chunks)
        v = src_ref[...].astype(dst_ref.dtype)
        dst_ref[...] = jnp.where(is_pad, jnp.zeros_like(v), v)
    else:
        o_ref, zbuf = rest
    tm = h_ref.shape[0]
    tn = o_ref.shape[1]
    per_seq = SEQ // tm

    @pl.when((pl.program_id(1) % per_seq) == 0)
    def _():
        zbuf[tm:tm + Z_HALO, :] = jnp.zeros((Z_HALO, zbuf.shape[1]), F32)

    z = jnp.dot(h_ref[...], w_ref[...], preferred_element_type=F32)
    zbuf[0:Z_HALO, :] = zbuf[tm:tm + Z_HALO, :]
    zbuf[Z_HALO:Z_HALO + tm, :] = z
    conv = c_ref[FFN_CONV_K - 1:FFN_CONV_K, :] * z
    for k in range(FFN_CONV_K - 1):
        s = FFN_CONV_K - 1 - k
        conv = conv + c_ref[k:k + 1, :] * zbuf[Z_HALO - s:Z_HALO - s + tm, :]
    o_ref[...] = (jax.nn.silu(conv[:, :tn]) * conv[:, tn:]).astype(o_ref.dtype)


def _ffn_up(h, up_b, l_up, conv_p, l, tm=TM, tn=FFN_TN, cast_src=None, cast_layer=None):
    t, d = h.shape
    grid = (FFN_PAD // tn, t // tm)
    in_specs = [
        pl.BlockSpec((tm, d), lambda j, i: (i, 0)),
        pl.BlockSpec((None, d, 2 * tn), lambda j, i: (l_up, 0, j)),
        pl.BlockSpec((None, FFN_CONV_K, 2 * tn), lambda j, i: (l, 0, j)),
    ]
    out_specs = pl.BlockSpec((tm, tn), lambda j, i: (i, j))
    out_shape = jax.ShapeDtypeStruct((t, FFN_PAD), BF16)
    operands = [h, up_b, conv_p]
    chunks = 0
    if cast_src is not None:
        n_blocks = 2 * FFN_PAD_BLOCKS
        chunks = (grid[0] * grid[1]) // n_blocks
        rows = cast_src.shape[1]
        assert chunks * n_blocks == grid[0] * grid[1] and rows % (chunks * 2 * SUBLANES) == 0
        slab = rows // chunks

        def step(j, i):
            return j * grid[1] + i

        in_specs.append(pl.BlockSpec(
            (None, slab, MXU_COLS),
            lambda j, i: (cast_layer, step(j, i) % chunks, _ffn_up_src(step(j, i) // chunks)[0])))
        out_specs = [out_specs, pl.BlockSpec(
            (None, slab, MXU_COLS), lambda j, i: (0, step(j, i) % chunks, step(j, i) // chunks))]
        out_shape = [out_shape, jax.ShapeDtypeStruct((1, rows, 2 * FFN_PAD), BF16)]
        operands.append(cast_src)
    return pl.pallas_call(
        functools.partial(_ffn_up_kernel, n_time_tiles=grid[1], cast_row_chunks=chunks),
        grid=grid,
        in_specs=in_specs,
        out_specs=out_specs,
        out_shape=out_shape,
        scratch_shapes=[pltpu.VMEM((Z_HALO + tm, 2 * tn), F32)],
        compiler_params=_cparams(("arbitrary", "arbitrary")),
        name="ffn_up",
    )(*operands)


def kernel(x, c, ada_w, ada_b, norm_mix_g, w_in, pool_w, pool_scale, sgu_ln_g, sgu_ln_b, sgu_w,
           sgu_b, sconv_w, conf_dw_w, conf_dw_b, conf_ln_g, conf_ln_b, gate_b, w_branch, w_out,
           norm_ffn_g, ffn_up, ffn_conv, ffn_down, final_g):
    bsz, seq, d = x.shape
    n_layers = ada_w.shape[0]
    assert (seq, d) == (SEQ, D_MODEL)
    x2 = x.reshape(bsz * seq, d)

    mod = _ada(c, ada_w, ada_b)
    conv_p = _reblock_ffn_conv(ffn_conv)
    w_in_b = _cast_bf16(w_in, 512, 4096, first=0, count=1)
    w_branch_b = _cast_bf16(w_branch, 512, 4096, first=0, count=1)
    w_out_b = _cast_bf16(w_out, 512, 4096, first=0, count=1)
    up_b = _cast_ffn_up(ffn_up, first=0, count=1)
    down_b = _cast_ffn_down(ffn_down, first=0, count=1)

    for l in range(n_layers):
        nxt = l + 1 if l + 1 < n_layers else None
        mod3 = mod[l].reshape(bsz, 1, 6 * d)
        mixer_params = (pool_w[l].astype(BF16), pool_scale[l].reshape(1, -1),
                        sgu_ln_g[l].reshape(1, -1), sgu_ln_b[l].reshape(1, -1), sgu_w[l],
                        sgu_b[l][:, :, None], sconv_w[l],
                        jnp.broadcast_to(conf_dw_w[l][:, None, :], (CONF_K, SUBLANES, W_MIX)),
                        conf_dw_b[l].reshape(1, -1), conf_ln_g[l].reshape(1, -1),
                        conf_ln_b[l].reshape(1, -1))

        h = _norm_mod(x2, norm_mix_g[l], mod3, sc_idx=1, sh_idx=0)
        if nxt is None:
            p = _mix_proj(h, w_in_b, 0)
            y = _mixers(p, *mixer_params)
        else:
            p, w_branch_nxt, w_out_nxt = _mix_proj(h, w_in_b, 0, cast_srcs=(w_branch, w_out),
                                                   cast_layer=nxt)
            y, down_nxt = _mixers(p, *mixer_params, cast_src=ffn_down, cast_layer=nxt)
        merged = _gate_lift(h, y, w_in_b, 0, w_branch_b, 0, gate_b, l)
        x2 = _resid_mm(merged, w_out_b, x2, mod3, 0, g_idx=2, tm=TM, tn=TN_OUT, tk=d)

        h = _norm_mod(x2, norm_ffn_g[l], mod3, sc_idx=4, sh_idx=3)
        if nxt is None:
            act = _ffn_up(h, up_b, 0, conv_p, l)
            x2 = _resid_mm(act, down_b, x2, mod3, 0, g_idx=5, tm=TM, tn=TN_DOWN, tk=TK_DOWN)
        else:
            act, up_b = _ffn_up(h, up_b, 0, conv_p, l, cast_src=ffn_up, cast_layer=nxt)
            x2, w_in_b = _resid_mm(act, down_b, x2, mod3, 0, g_idx=5, tm=TM, tn=TN_DOWN,
                                   tk=TK_DOWN, cast_src=w_in, cast_layer=nxt)
            w_branch_b, w_out_b, down_b = w_branch_nxt, w_out_nxt, down_nxt

    return _final_norm(x2, final_g).reshape(bsz, seq, d)
```
